```python
import math
import functools
import jax
import jax.numpy as jnp
from jax import lax
import numpy as np

D_MODEL = 1024
BATCH = 2
SEQ = 8192
DEPTH = 1
DEC_BATCH = 128
DEC_SEQ = 8
PAST_LEN = 8192
PAGE_SIZE = 128

H_M = 4
DK_M = 128
DV_M = 128
M_CHUNK = 64
H_A = 4
DH_A = 128
H_I = 8
D_IDX = 64
TOPK_MAX = 256
Q_BLOCK = 128
ROPE_THETA = 10000.0
N_KEYS = 128
N_EXPERTS = N_KEYS * N_KEYS
H_P = 8
D_PKEY = 128
P_TOPK = 16
P_BLOCK = 256
EPS = 1e-6

MIX_SPLITS = (H_M * DK_M, H_M * DK_M, H_M * DV_M, H_M, H_M, H_M * DV_M,
              H_A * DH_A, H_A * DH_A, H_A * DH_A, H_I * D_IDX, D_IDX, H_I, 2 * D_MODEL)
D_IN = sum(MIX_SPLITS)

kernel_name = 'hybrid_mlstm_dsa_peer_step'


def rmsnorm(x, g):
    xf = x.astype(jnp.float32)
    y = xf * lax.rsqrt(jnp.mean(xf * xf, axis=-1, keepdims=True) + EPS)
    return (y * g.astype(jnp.float32)).astype(x.dtype)


def rope(x, pos):
    d = x.shape[-1]
    inv = ROPE_THETA ** (-jnp.arange(0, d, 2, dtype=jnp.float32) / d)
    ang = pos.astype(jnp.float32)[:, None] * inv[None, :]
    cos = jnp.cos(ang)[:, None, :]
    sin = jnp.sin(ang)[:, None, :]
    xf = x.astype(jnp.float32)
    x1, x2 = xf[..., : d // 2], xf[..., d // 2:]
    return jnp.concatenate([x1 * cos - x2 * sin, x2 * cos + x1 * sin], axis=-1).astype(x.dtype)


def split_cols(z):
    parts, o = [], 0
    for s in MIX_SPLITS:
        parts.append(z[..., o:o + s])
        o += s
    return parts


def mix_project(xn, pos, w_in, b_mgate):
    B, T, _ = xn.shape
    mq, mk, mv, mi, mf, mo, aq, ak, av, iq, ik, iw, gates = split_cols(xn @ w_in)
    mq = mq.reshape(B, T, H_M, DK_M) * (DK_M ** -0.5)
    mk = mk.reshape(B, T, H_M, DK_M)
    mv = mv.reshape(B, T, H_M, DV_M)
    i_pre = (mi + b_mgate[:H_M]).astype(jnp.float32)
    log_f = jax.nn.log_sigmoid((mf + b_mgate[H_M:]).astype(jnp.float32))
    aq = rope(aq.reshape(B, T, H_A, DH_A), pos)
    ak = rope(ak.reshape(B, T, H_A, DH_A), pos)
    av = av.reshape(B, T, H_A, DH_A)
    iq = rope(iq.reshape(B, T, H_I, D_IDX), pos) * (D_IDX ** -0.5)
    ik = rope(ik[:, :, None, :], pos)[:, :, 0, :]
    iw = iw * (H_I ** -0.5)
    return (mq, mk, mv, i_pre, log_f, mo, aq, ak, av, iq, ik, iw, gates)


def mlstm_chunkwise(q, k, v, i_pre, log_f, C0, n0, m0):
    B, T, H, _ = q.shape
    c = math.gcd(M_CHUNK, T)
    nc = T // c

    def chunks(a):
        a = a.astype(jnp.float32).reshape((B, nc, c) + a.shape[2:])
        return jnp.moveaxis(jnp.moveaxis(a, 1, 0), 3, 2)

    tri = jnp.tril(jnp.ones((c, c), dtype=bool))

    def step(carry, inp):
        C, n, m = carry
        qc, kc, vc, ic, fc = inp
        b = jnp.cumsum(fc, axis=-1)
        D = jnp.where(tri, b[..., :, None] - b[..., None, :] + ic[..., None, :], -jnp.inf)
        m_inter = b + m[..., None]
        m_t = jnp.maximum(m_inter, jnp.max(D, axis=-1))
        w_inter = jnp.exp(m_inter - m_t)
        S = jnp.einsum('bhtd,bhsd->bhts', qc, kc) * jnp.exp(D - m_t[..., None])
        num = w_inter[..., None] * jnp.einsum('bhvd,bhtd->bhtv', C, qc) + jnp.einsum('bhts,bhsv->bhtv', S, vc)
        den = w_inter * jnp.einsum('bhd,bhtd->bht', n, qc) + jnp.sum(S, axis=-1)
        h = num / jnp.maximum(jnp.abs(den), jnp.exp(-m_t))[..., None]
        m_new = m_t[..., -1]
        g_state = jnp.exp(b[..., -1] + m - m_new)
        g_tok = jnp.exp(b[..., -1:] - b + ic - m_new[..., None])
        C_new = g_state[..., None, None] * C + jnp.einsum('bhs,bhsv,bhsd->bhvd', g_tok, vc, kc)
        n_new = g_state[..., None] * n + jnp.einsum('bhs,bhsd->bhd', g_tok, kc)
        return (C_new, n_new, m_new), h

    init = (C0.astype(jnp.float32), n0.astype(jnp.float32), m0.astype(jnp.float32))
    xs = (chunks(q), chunks(k), chunks(v), chunks(i_pre), chunks(log_f))
    (C, n, m), h = lax.scan(step, init, xs)
    h = jnp.moveaxis(jnp.moveaxis(h, 0, 1), 3, 2).reshape(B, T, H, v.shape[-1])
    return h, C, n, m


def dsa_prompt(q, k, v, qi, ki, wi):
    B, S = q.shape[:2]
    topk = min(TOPK_MAX, S // 4)
    nb = S // Q_BLOCK
    kpos = jnp.arange(S)
    kif = ki.astype(jnp.float32)

    def blocks(a):
        return jnp.moveaxis(a.reshape((B, nb, Q_BLOCK) + a.shape[2:]), 1, 0)

    def one_block(args):
        blk, qb, qib, wib = args
        qpos = blk * Q_BLOCK + jnp.arange(Q_BLOCK)
        causal = kpos[None, :] <= qpos[:, None]
        idx = jax.nn.relu(jnp.einsum('bqhd,bsd->bqhs', qib.astype(jnp.float32), kif))
        score = jnp.einsum('bqh,bqhs->bqs', wib.astype(jnp.float32), idx)
        score = jnp.where(causal[None], score, -jnp.inf)
        top, sel = lax.top_k(score, topk)
        valid = jnp.isfinite(top)
        ksel = jax.vmap(lambda kk, ss: kk[ss])(k, sel)
        vsel = jax.vmap(lambda vv, ss: vv[ss])(v, sel)
        logits = jnp.einsum('bqhd,bqkhd->bhqk', qb, ksel).astype(jnp.float32) * (DH_A ** -0.5)
        logits = jnp.where(valid[:, None], logits, -jnp.inf)
        p = jax.nn.softmax(logits, axis=-1).astype(v.dtype)
        return jnp.einsum('bhqk,bqkhd->bqhd', p, vsel)

    out = lax.map(one_block, (jnp.arange(nb), blocks(q), blocks(qi), blocks(wi)))
    return jnp.moveaxis(out, 0, 1).reshape(B, S, H_A * DH_A)


def dsa_sample(q, k_new, v_new, qi, ki_new, wi, cache_k, cache_v, cache_kidx, page_table):
    Bd, T = q.shape[:2]
    n_pages = page_table.shape[1]
    past = n_pages * PAGE_SIZE
    L = past + T
    topk = min(TOPK_MAX, L // 4)
    ki_past = cache_kidx[page_table].reshape(Bd, past, D_IDX)
    ki_all = jnp.concatenate([ki_past.astype(jnp.float32), ki_new.astype(jnp.float32)], axis=1)
    new_pos = past + jnp.arange(T)
    causal = jnp.arange(L)[None, :] <= new_pos[:, None]
    idx = jax.nn.relu(jnp.einsum('bthd,bsd->bths', qi.astype(jnp.float32), ki_all))
    score = jnp.einsum('bth,bths->bts', wi.astype(jnp.float32), idx)
    score = jnp.where(causal[None], score, -jnp.inf)
    top, sel = lax.top_k(score, topk)
    valid = jnp.isfinite(top)
    from_past = sel < past
    sel_p = jnp.where(from_past, sel, 0)
    phys = jax.vmap(lambda pt, s: pt[s])(page_table, sel_p // PAGE_SIZE)
    off = sel_p % PAGE_SIZE
    k_past_sel = cache_k[phys, off]
    v_past_sel = cache_v[phys, off]
    past_valid = valid & from_past
    new_sel = jnp.any((sel[..., :, None] == new_pos) & valid[..., None], axis=2)
    lp = jnp.einsum('bthd,btkhd->bhtk', q, k_past_sel).astype(jnp.float32)
    ln = jnp.einsum('bthd,bshd->bhts', q, k_new).astype(jnp.float32)
    logits = jnp.concatenate([lp, ln], axis=-1) * (DH_A ** -0.5)
    mask = jnp.concatenate([past_valid, new_sel], axis=-1)[:, None]
    p = jax.nn.softmax(jnp.where(mask, logits, -jnp.inf), axis=-1)
    p_past = p[..., :topk].astype(v_new.dtype)
    p_new = p[..., topk:].astype(v_new.dtype)
    out = jnp.einsum('bhtk,btkhd->bthd', p_past, v_past_sel) + jnp.einsum('bhts,bshd->bthd', p_new, v_new)
    return out.reshape(Bd, T, H_A * DH_A)


def merge_branches(h_m, o_pre, att, gates, g_mnorm, w_a, w_b, w_o):
    B, T = att.shape[:2]
    hn = rmsnorm(h_m, g_mnorm.reshape(H_M, DV_M)).reshape(B, T, H_M * DV_M)
    hn = (jax.nn.sigmoid(o_pre.astype(jnp.float32)) * hn).astype(att.dtype)
    ga = jax.nn.sigmoid(gates[..., :D_MODEL])
    gb = jax.nn.sigmoid(gates[..., D_MODEL:])
    merged = ga * (hn @ w_a) + gb * (att @ w_b)
    return merged @ w_o


def mixer_block(h, pos, attend, C0, n0, m0, g_mix, w_in, b_mgate, g_mnorm, w_a, w_b, w_o):
    (mq, mk, mv, ip, lf, mo, aq, ak, av, iq, ik, iw, gates) = mix_project(rmsnorm(h, g_mix), pos, w_in, b_mgate)
    hm, C, n, m = mlstm_chunkwise(mq, mk, mv, ip, lf, C0, n0, m0)
    att = attend(aq, ak, av, iq, ik, iw)
    h = h + merge_branches(hm, mo, att, gates, g_mnorm, w_a, w_b, w_o)
    return h, ak, av, ik, C, n, m


def peer_ffn(x, w_pq, sub_keys, peer_u, peer_v):
    N, D = x.shape
    pad = (-N) % P_BLOCK
    xb = jnp.pad(x, ((0, pad), (0, 0))).reshape(-1, P_BLOCK, D)

    def one_block(xt):
        qh = (xt @ w_pq).reshape(P_BLOCK, H_P, 2, D_PKEY // 2)
        s = jnp.einsum('nhpd,hpkd->nhpk', qh, sub_keys).astype(jnp.float32)
        s1, i1 = lax.top_k(s[:, :, 0], P_TOPK)
        s2, i2 = lax.top_k(s[:, :, 1], P_TOPK)
        cand = (s1[..., :, None] + s2[..., None, :]).reshape(P_BLOCK, H_P, P_TOPK * P_TOPK)
        cid = (i1[..., :, None] * N_KEYS + i2[..., None, :]).reshape(P_BLOCK, H_P, P_TOPK * P_TOPK)
        top, j = lax.top_k(cand, P_TOPK)
        eid = jnp.take_along_axis(cid, j, axis=-1)
        g = jax.nn.softmax(top, axis=-1)
        a = jax.nn.gelu(jnp.einsum('nd,nhkd->nhk', xt, peer_u[eid]).astype(jnp.float32), approximate=False)
        return jnp.einsum('nhk,nhkd->nd', (g * a).astype(xt.dtype), peer_v[eid])

    return lax.map(one_block, xb).reshape(-1, D)[:N]


def channel_block(h, g_ffn, w_pq, sub_keys, peer_u, peer_v):
    B, T, D = h.shape
    return h + peer_ffn(rmsnorm(h, g_ffn).reshape(B * T, D), w_pq, sub_keys, peer_u, peer_v).reshape(B, T, D).astype(h.dtype)


def setup_inputs(seed: int = 0) -> dict:
    key = jax.random.key(seed)
    ks = jax.random.split(key, 24)
    f32 = jnp.float32
    n_pages = PAST_LEN // PAGE_SIZE
    n_pool = (DEC_BATCH * n_pages * 5) // 4

    def nrm(k, shape, s):
        return jax.random.normal(k, shape, f32) * s

    x_prompt = nrm(ks[0], (BATCH, SEQ, D_MODEL), 1.0)
    x_sample = nrm(ks[1], (DEC_BATCH, DEC_SEQ, D_MODEL), 1.0)
    cache_k = nrm(ks[2], (DEPTH, n_pool, PAGE_SIZE, H_A, DH_A), 1.0)
    cache_v = nrm(ks[3], (DEPTH, n_pool, PAGE_SIZE, H_A, DH_A), 1.0)
    cache_kidx = nrm(ks[4], (DEPTH, n_pool, PAGE_SIZE, D_IDX), 1.0)
    page_table = jax.random.permutation(ks[5], n_pool)[: DEC_BATCH * n_pages].reshape(DEC_BATCH, n_pages).astype(jnp.int32)
    state_C = nrm(ks[6], (DEPTH, DEC_BATCH, H_M, DV_M, DK_M), 0.3)
    state_n = nrm(ks[7], (DEPTH, DEC_BATCH, H_M, DK_M), 1.0)
    state_m = nrm(ks[8], (DEPTH, DEC_BATCH, H_M), 1.0)
    g_mix = 1.0 + nrm(ks[9], (DEPTH, D_MODEL), 0.01)
    w_in = nrm(ks[10], (DEPTH, D_MODEL, D_IN), D_MODEL ** -0.5)
    b_mgate = jnp.concatenate([nrm(ks[11], (DEPTH, H_M), 0.1),
                               jnp.linspace(3.0, 6.0, H_M, dtype=f32)[None, :] + nrm(ks[12], (DEPTH, H_M), 0.1)], axis=-1)
    g_mnorm = 1.0 + nrm(ks[13], (DEPTH, H_M * DV_M), 0.01)
    w_a = nrm(ks[14], (DEPTH, H_M * DV_M, D_MODEL), (H_M * DV_M) ** -0.5)
    w_b = nrm(ks[15], (DEPTH, H_A * DH_A, D_MODEL), (H_A * DH_A) ** -0.5)
    w_o = nrm(ks[16], (DEPTH, D_MODEL, D_MODEL), D_MODEL ** -0.5)
    g_ffn = 1.0 + nrm(ks[17], (DEPTH, D_MODEL), 0.01)
    w_pq = nrm(ks[18], (DEPTH, D_MODEL, H_P * D_PKEY), D_MODEL ** -0.5)
    sub_keys = nrm(ks[19], (DEPTH, H_P, 2, N_KEYS, D_PKEY // 2), (D_PKEY // 2) ** -0.5)
    peer_u = nrm(ks[20], (DEPTH, N_EXPERTS, D_MODEL), D_MODEL ** -0.5)
    peer_v = nrm(ks[21], (DEPTH, N_EXPERTS, D_MODEL), (H_P * P_TOPK) ** -0.5)
    g_final = 1.0 + nrm(ks[22], (D_MODEL,), 0.01)
    return {'x_prompt': x_prompt, 'x_sample': x_sample, 'cache_k': cache_k, 'cache_v': cache_v,
            'cache_kidx': cache_kidx, 'page_table': page_table, 'state_C': state_C, 'state_n': state_n,
            'state_m': state_m, 'g_mix': g_mix, 'w_in': w_in, 'b_mgate': b_mgate, 'g_mnorm': g_mnorm,
            'w_a': w_a, 'w_b': w_b, 'w_o': w_o, 'g_ffn': g_ffn, 'w_pq': w_pq, 'sub_keys': sub_keys,
            'peer_u': peer_u, 'peer_v': peer_v, 'g_final': g_final}


def reference(x_prompt, x_sample, cache_k, cache_v, cache_kidx, page_table, state_C, state_n, state_m,
              g_mix, w_in, b_mgate, g_mnorm, w_a, w_b, w_o, g_ffn, w_pq, sub_keys, peer_u, peer_v, g_final):
    past = page_table.shape[1] * PAGE_SIZE
    Bp, Sp = x_prompt.shape[:2]
    pos_p = jnp.arange(Sp)
    pos_s = past + jnp.arange(x_sample.shape[1])
    hp, hs = x_prompt, x_sample
    kp, vp, kip, Cp, npl, mp = [], [], [], [], [], []
    ksl, vsl, kis, Cs, nsl, msl = [], [], [], [], [], []
    for l in range(DEPTH):
        layer_w = (g_mix[l], w_in[l], b_mgate[l], g_mnorm[l], w_a[l], w_b[l], w_o[l])
        ffn_w = (g_ffn[l], w_pq[l], sub_keys[l], peer_u[l], peer_v[l])
        C0 = jnp.zeros((Bp, H_M, DV_M, DK_M), jnp.float32)
        n0 = jnp.zeros((Bp, H_M, DK_M), jnp.float32)
        m0 = jnp.zeros((Bp, H_M), jnp.float32)
        hp, ak, av, ik, C, n, m = mixer_block(hp, pos_p, dsa_prompt, C0, n0, m0, *layer_w)
        hp = channel_block(hp, *ffn_w)
        kp.append(ak); vp.append(av); kip.append(ik); Cp.append(C); npl.append(n); mp.append(m)
        attend_s = functools.partial(dsa_sample, cache_k=cache_k[l], cache_v=cache_v[l],
                                     cache_kidx=cache_kidx[l], page_table=page_table)
        hs, ak, av, ik, C, n, m = mixer_block(hs, pos_s, attend_s, state_C[l], state_n[l], state_m[l], *layer_w)
        hs = channel_block(hs, *ffn_w)
        ksl.append(ak); vsl.append(av); kis.append(ik); Cs.append(C); nsl.append(n); msl.append(m)
    y_prompt = rmsnorm(hp, g_final)
    y_sample = rmsnorm(hs, g_final)
    k_prompt, v_prompt, kidx_prompt = jnp.stack(kp), jnp.stack(vp), jnp.stack(kip)
    C_prompt, n_prompt, m_prompt = jnp.stack(Cp), jnp.stack(npl), jnp.stack(mp)
    k_sample, v_sample, kidx_sample = jnp.stack(ksl), jnp.stack(vsl), jnp.stack(kis)
    C_sample, n_sample, m_sample = jnp.stack(Cs), jnp.stack(nsl), jnp.stack(msl)
    return (y_prompt, y_sample, k_prompt, v_prompt, kidx_prompt, C_prompt, n_prompt, m_prompt,
            k_sample, v_sample, kidx_sample, C_sample, n_sample, m_sample)
```

```python
import functools
import math

import jax
import jax.numpy as jnp
from jax import lax
from jax.experimental import pallas as pl
from jax.experimental.pallas import tpu as pltpu

F32 = jnp.float32
BF16 = jnp.bfloat16
I32 = jnp.int32

D_MODEL = 1024
PAGE_SIZE = 128
H_M, DK_M, DV_M, M_CHUNK = 4, 128, 128, 64
H_A, DH_A = 4, 128
H_I, D_IDX = 8, 64
TOPK_MAX = 256
Q_BLOCK = 128
ROPE_THETA = 10000.0
N_KEYS = 128
H_P, D_PKEY, P_TOPK = 8, 128, 16
EPS = 1e-6

MIX_SPLITS = (H_M * DK_M, H_M * DK_M, H_M * DV_M, H_M, H_M, H_M * DV_M,
              H_A * DH_A, H_A * DH_A, H_A * DH_A, H_I * D_IDX, D_IDX, H_I, 2 * D_MODEL)

LANES = 128
VMEM_LIMIT_BYTES = 56 * 1024 * 1024
NEG_INF_KEY = -2139095041
INT_MIN = -2147483648
NEG_BIG = -1e30

SM_IK, SM_IP, SM_LF, SM_IW, SM_END = 0, 64, 68, 72, 80
W_MQ, W_MK, W_MV, W_MO, W_AQ, W_AK, W_AV, W_IQ, W_G, W_SM, W_COLS = (
    0, 512, 1024, 1536, 2048, 2560, 3072, 3584, 4096, 6144, 6272)


def _dot(a, b):
    return jnp.dot(a, b, preferred_element_type=F32)


def _dot_nt(a, b):
    return lax.dot_general(a, b, (((1,), (1,)), ((), ())), preferred_element_type=F32)


def _dot_tn(a, b):
    return lax.dot_general(a, b, (((0,), (0,)), ((), ())), preferred_element_type=F32)


def _okey(x):
    i = lax.bitcast_convert_type(x, I32)
    return i ^ ((i >> 31) & jnp.int32(0x7FFFFFFF))


def _cparams(sem):
    return pltpu.CompilerParams(dimension_semantics=sem, vmem_limit_bytes=VMEM_LIMIT_BYTES)


def _inproj_kernel(x_ref, g_ref, w_ref, bias_ref, cosa_ref, sina_ref, cosi_ref, sini_ref,
                   mq_ref, mk_ref, mv_ref, mo_ref, aq_ref, ak_ref, av_ref, akb_ref, avb_ref,
                   iq_ref, sg_ref, sm_ref, ik_ref, ik2_ref):
    x = x_ref[...]
    ms = jnp.mean(x * x, axis=-1, keepdims=True)
    xn = ((x * lax.rsqrt(ms + EPS)) * g_ref[...]).astype(BF16)

    def proj(lo, width):
        return _dot(xn, w_ref[:, lo:lo + width])

    cosa, sina = cosa_ref[...], sina_ref[...]
    cosi, sini = cosi_ref[...], sini_ref[...]
    lane = lax.broadcasted_iota(I32, (x.shape[0], LANES), 1)
    first_half = (lane % 64) < 32

    def rope128(z):
        outs = []
        for h in range(z.shape[1] // LANES):
            zh = z[:, h * LANES:(h + 1) * LANES]
            outs.append(zh * cosa + pltpu.roll(zh, 64, 1) * sina)
        return jnp.concatenate(outs, axis=1) if len(outs) > 1 else outs[0]

    def rope64_slab(zh):
        rot = jnp.where(first_half, pltpu.roll(zh, 96, 1), pltpu.roll(zh, 32, 1))
        return zh * cosi + rot * sini

    mq_ref[...] = (proj(W_MQ, 512) * (DK_M ** -0.5)).astype(BF16)
    mk_ref[...] = proj(W_MK, 512).astype(BF16)
    mv_ref[...] = proj(W_MV, 512).astype(BF16)
    mo_ref[...] = proj(W_MO, 512)
    aq_ref[...] = rope128(proj(W_AQ, 512)).astype(BF16)
    ak = rope128(proj(W_AK, 512))
    ak_ref[...] = ak
    akb_ref[...] = ak.astype(BF16)
    av = proj(W_AV, 512)
    av_ref[...] = av
    avb_ref[...] = av.astype(BF16)
    ziq = proj(W_IQ, 512)
    iq = jnp.concatenate([rope64_slab(ziq[:, h * LANES:(h + 1) * LANES]) for h in range(4)], axis=1)
    iq_ref[...] = (iq * (D_IDX ** -0.5)).astype(BF16)
    sg_ref[...] = jax.nn.sigmoid(proj(W_G, 2 * D_MODEL))
    zs = proj(W_SM, LANES)
    ikr = rope64_slab(zs)
    zb = zs + bias_ref[...]
    logsig = jnp.minimum(zb, 0.0) - jnp.log1p(jnp.exp(-jnp.abs(zb)))
    sm = jnp.where(lane < SM_IP, ikr,
                   jnp.where(lane < SM_LF, zb,
                             jnp.where(lane < SM_IW, logsig,
                                       jnp.where(lane < SM_END, zs * (H_I ** -0.5), 0.0))))
    sm_ref[...] = sm
    ik_ref[...] = ikr[:, :D_IDX]
    ik2_ref[...] = jnp.where(lane < D_IDX, ikr, pltpu.roll(ikr, 64, 1)).astype(BF16)


def _prep_w_in(w_in):
    parts, o = [], 0
    for s in MIX_SPLITS:
        parts.append(w_in[:, o:o + s])
        o += s
    mq, mk, mv, mi, mf, mo, aq, ak, av, iq, ik, iw, gates = parts
    pad = jnp.zeros((w_in.shape[0], LANES - SM_END), w_in.dtype)
    small = jnp.concatenate([ik, mi, mf, iw, pad], axis=1)
    return jnp.concatenate([mq, mk, mv, mo, aq, ak, av, iq, gates, small], axis=1).astype(BF16)


def _rope_tables(pos):
    pos = pos.astype(F32)[:, None]

    def tab(d):
        inv = ROPE_THETA ** (-jnp.arange(0, d, 2, dtype=F32) / d)
        ang = pos * inv[None, :]
        c, s = jnp.cos(ang), jnp.sin(ang)
        cos = jnp.concatenate([c, c], axis=1)
        sin = jnp.concatenate([-s, s], axis=1)
        reps = LANES // d
        return jnp.tile(cos, (1, reps)), jnp.tile(sin, (1, reps))

    cosa, sina = tab(DH_A)
    cosi, sini = tab(D_IDX)
    return cosa, sina, cosi, sini


def _inproj(x2d, g_mix, wp, bias_row, tables, tm):
    n = x2d.shape[0]
    t_tab = tables[0].shape[0]
    assert n % tm == 0 and t_tab % tm == 0
    nt = t_tab // tm
    row = lambda i: (i, 0)
    tabspec = pl.BlockSpec((tm, LANES), lambda i: (i % nt, 0))
    full = lambda shape: pl.BlockSpec(shape, lambda i: (0, 0))
    out_defs = [
        (512, BF16), (512, BF16), (512, BF16), (512, F32), (512, BF16), (512, F32), (512, F32),
        (512, BF16), (512, BF16), (512, BF16), (2 * D_MODEL, F32), (LANES, F32), (D_IDX, F32), (LANES, BF16)]
    return pl.pallas_call(
        _inproj_kernel,
        grid=(n // tm,),
        in_specs=[pl.BlockSpec((tm, D_MODEL), row), full((1, D_MODEL)), full((D_MODEL, W_COLS)),
                  full((1, LANES)), tabspec, tabspec, tabspec, tabspec],
        out_specs=[pl.BlockSpec((tm, w), row) for w, _ in out_defs],
        out_shape=[jax.ShapeDtypeStruct((n, w), dt) for w, dt in out_defs],
        compiler_params=_cparams(("parallel",)),
        name="inproj",
    )(x2d, g_mix.reshape(1, D_MODEL), wp, bias_row, *tables)


def _mlstm_kernel(q_ref, k_ref, v_ref, sm_ref, mo_ref, gm_ref, c0_ref, n0_ref, m0_ref,
                  hn_ref, c_ref, n_ref, m_ref, *, c, nchunks):
    @pl.when(pl.program_id(1) == 0)
    def _():
        c_ref[...] = c0_ref[...]
        n_ref[...] = n0_ref[...]
        m_ref[...] = m0_ref[...]

    ri = lax.broadcasted_iota(I32, (c, c), 0)
    ci = lax.broadcasted_iota(I32, (c, c), 1)
    eye = ri == ci
    tri = ri >= ci

    def to_row(col):
        return jnp.sum(jnp.where(eye, jnp.broadcast_to(col, (c, c)), 0.0), axis=0, keepdims=True)

    def chunk(rows):
        q = q_ref[0, rows, :]
        k = k_ref[0, rows, :]
        v = v_ref[0, rows, :]
        sm = sm_ref[0, rows, :]
        mo = mo_ref[0, rows, :]
        for h in range(H_M):
            sl = slice(h * LANES, (h + 1) * LANES)
            qh, kh, vh = q[:, sl], k[:, sl], v[:, sl]
            ipc = sm[:, SM_IP + h:SM_IP + h + 1]
            lfc = sm[:, SM_LF + h:SM_LF + h + 1]
            lf_row = to_row(lfc)
            ip_row = to_row(ipc)
            b_col = jnp.sum(jnp.where(tri, jnp.broadcast_to(lf_row, (c, c)), 0.0), axis=1, keepdims=True)
            b_row = to_row(b_col)
            m_prev = m_ref[0, h][:, :1]
            n_prev = n_ref[0, h]
            c_prev = c_ref[0, h]
            m_inter = b_col + m_prev
            dmat = jnp.where(tri, b_col - b_row + ip_row, -jnp.inf)
            m_t = jnp.maximum(m_inter, jnp.max(dmat, axis=1, keepdims=True))
            w_inter = jnp.exp(m_inter - m_t)
            s = _dot_nt(qh, kh) * jnp.exp(dmat - m_t)
            num = w_inter * _dot_nt(qh, c_prev.astype(BF16)) + _dot(s.astype(BF16), vh)
            den = (w_inter * jnp.sum(qh.astype(F32) * n_prev, axis=1, keepdims=True)
                   + jnp.sum(s, axis=1, keepdims=True))
            hh = num / jnp.maximum(jnp.abs(den), jnp.exp(-m_t))
            m_new = m_t[c - 1:c, :]
            b_last = b_col[c - 1:c, :]
            g_state = jnp.exp(b_last + m_prev - m_new)
            g_tok = jnp.exp(b_last - b_col + ipc - m_new)
            gv = (g_tok * vh.astype(F32)).astype(BF16)
            c_ref[0, h] = g_state * c_prev + _dot_tn(gv, kh)
            n_ref[0, h] = g_state * n_prev + jnp.sum(g_tok * kh.astype(F32), axis=0, keepdims=True)
            m_ref[0, h] = jnp.broadcast_to(m_new, (1, LANES))
            y = hh * lax.rsqrt(jnp.mean(hh * hh, axis=1, keepdims=True) + EPS) * gm_ref[:, sl]
            hn_ref[0, rows, sl] = (jax.nn.sigmoid(mo[:, sl]) * y).astype(BF16)

    if nchunks == 1:
        chunk(slice(0, c))
    else:
        def body(i, carry):
            chunk(pl.ds(pl.multiple_of(i * c, c), c))
            return carry
        lax.fori_loop(0, nchunks, body, 0)


def _mlstm(mq, mk, mv, sm, mo, g_mnorm, c0, n0, m0, nb, t):
    c = math.gcd(M_CHUNK, t)
    tb = min(t, 8 * c)
    assert t % tb == 0
    r3 = lambda a: a.reshape(nb, t, a.shape[-1])
    n0b = n0.reshape(nb, H_M, 1, DK_M).astype(F32)
    m0b = jnp.broadcast_to(m0.astype(F32)[:, :, None, None], (nb, H_M, 1, LANES))
    tok = lambda w: pl.BlockSpec((1, tb, w), lambda b, j: (b, j, 0))
    st4 = lambda shape: pl.BlockSpec(shape, lambda b, j: (b, 0, 0, 0))
    hn, cc, nn, mm = pl.pallas_call(
        functools.partial(_mlstm_kernel, c=c, nchunks=tb // c),
        grid=(nb, t // tb),
        in_specs=[tok(512), tok(512), tok(512), tok(LANES), tok(512),
                  pl.BlockSpec((1, H_M * DV_M), lambda b, j: (0, 0)),
                  st4((1, H_M, DV_M, DK_M)), st4((1, H_M, 1, DK_M)), st4((1, H_M, 1, LANES))],
        out_specs=[tok(512), st4((1, H_M, DV_M, DK_M)), st4((1, H_M, 1, DK_M)), st4((1, H_M, 1, LANES))],
        out_shape=[jax.ShapeDtypeStruct((nb, t, H_M * DV_M), BF16),
                   jax.ShapeDtypeStruct((nb, H_M, DV_M, DK_M), F32),
                   jax.ShapeDtypeStruct((nb, H_M, 1, DK_M), F32),
                   jax.ShapeDtypeStruct((nb, H_M, 1, LANES), F32)],
        compiler_params=_cparams(("parallel", "arbitrary")),
        name="mlstm",
    )(r3(mq), r3(mk), r3(mv), r3(sm), r3(mo), g_mnorm.reshape(1, H_M * DV_M),
      c0.astype(F32), n0b, m0b)
    return hn.reshape(nb * t, H_M * DV_M), cc, nn.reshape(nb, H_M, DK_M), mm[:, :, 0, 0]


def _kth_largest_key(count_ge, k, shape):
    kf = jnp.float32(k)
    r0 = jnp.where(count_ge(jnp.zeros(shape, I32)) >= kf, jnp.int32(0), jnp.int32(INT_MIN))

    def body(t, r):
        cand = r + lax.shift_left(jnp.int32(1), jnp.int32(30) - t)
        return jnp.where(count_ge(cand) >= kf, cand, r)

    return lax.fori_loop(0, 31, body, r0)


def _dsa_prompt_kernel(aq_ref, iq_ref, sm_ref, kb_ref, vb_ref, ik2_ref, out_ref, skey_ref, *, kc, topk):
    i = pl.program_id(1)
    qb = Q_BLOCK
    nk = (i * qb + qb + kc - 1) // kc
    iq = iq_ref[0]
    lane = lax.broadcasted_iota(I32, (qb, LANES), 1)
    zero = jnp.zeros((qb, LANES), BF16)
    lhs = []
    for hp in range(H_I // 2):
        slab = iq[:, hp * LANES:(hp + 1) * LANES]
        lhs.append(jnp.where(lane < D_IDX, slab, zero))
        lhs.append(jnp.where(lane >= D_IDX, slab, zero))
    sm = sm_ref[0]
    qpos = i * qb + lax.broadcasted_iota(I32, (qb, 1), 0)

    def score_body(cidx, carry):
        k0 = pl.multiple_of(cidx * kc, kc)
        kk = ik2_ref[0, pl.ds(k0, kc), :]
        sc = jnp.zeros((qb, kc), F32)
        for h in range(H_I):
            sc = sc + sm[:, SM_IW + h:SM_IW + h + 1] * jnp.maximum(_dot_nt(lhs[h], kk), 0.0)
        kpos = k0 + lax.broadcasted_iota(I32, (1, kc), 1)
        sc = jnp.where(kpos <= qpos, sc, -jnp.inf)
        skey_ref[:, pl.ds(k0, kc)] = _okey(sc)
        return carry

    lax.fori_loop(0, nk, score_body, 0)

    def count_ge(cand):
        def body(cidx, acc):
            k0 = pl.multiple_of(cidx * kc, kc)
            ks = skey_ref[:, pl.ds(k0, kc)]
            m = jnp.where(ks >= cand, 1.0, 0.0)
            part = m[:, :LANES]
            for t in range(1, kc // LANES):
                part = part + m[:, t * LANES:(t + 1) * LANES]
            return acc + part
        acc = lax.fori_loop(0, nk, body, jnp.zeros((qb, LANES), F32))
        return jnp.sum(acc, axis=1, keepdims=True)

    thr = _kth_largest_key(count_ge, topk, (qb, 1))
    thr = jnp.maximum(thr, jnp.int32(NEG_INF_KEY + 1))
    scale = DH_A ** -0.5
    aq = aq_ref[0]
    for h in range(H_A):
        sl = slice(h * LANES, (h + 1) * LANES)
        qh = aq[:, sl]

        def att_body(cidx, carry, sl=sl, qh=qh):
            m_i, l_i, acc = carry
            k0 = pl.multiple_of(cidx * kc, kc)
            kh = kb_ref[0, pl.ds(k0, kc), sl]
            vh = vb_ref[0, pl.ds(k0, kc), sl]
            sel = skey_ref[:, pl.ds(k0, kc)] >= thr
            lg = jnp.where(sel, _dot_nt(qh, kh) * scale, NEG_BIG)
            m_new = jnp.maximum(m_i, jnp.max(lg, axis=1, keepdims=True))
            alpha = jnp.exp(m_i - m_new)
            p = jnp.where(sel, jnp.exp(lg - m_new), 0.0)
            l_new = alpha * l_i + jnp.sum(p, axis=1, keepdims=True)
            acc = alpha * acc + _dot(p.astype(BF16), vh)
            return m_new, l_new, acc

        init = (jnp.full((qb, 1), NEG_BIG, F32), jnp.zeros((qb, 1), F32), jnp.zeros((qb, LANES), F32))
        _, l_f, acc_f = lax.fori_loop(0, nk, att_body, init)
        out_ref[0, :, sl] = (acc_f / l_f).astype(BF16)


def _dsa_prompt(aq, iq, sm, akb, avb, ik2, nb, s):
    topk = min(TOPK_MAX, s // 4)
    kc = min(512, s)
    assert s % kc == 0 and s % Q_BLOCK == 0 and kc % Q_BLOCK == 0
    r3 = lambda a: a.reshape(nb, s, a.shape[-1])
    qspec = lambda w: pl.BlockSpec((1, Q_BLOCK, w), lambda b, i: (b, i, 0))
    kspec = lambda w: pl.BlockSpec((1, s, w), lambda b, i: (b, 0, 0))
    out = pl.pallas_call(
        functools.partial(_dsa_prompt_kernel, kc=kc, topk=topk),
        grid=(nb, s // Q_BLOCK),
        in_specs=[qspec(512), qspec(512), qspec(LANES), kspec(512), kspec(512), kspec(LANES)],
        out_specs=qspec(512),
        out_shape=jax.ShapeDtypeStruct((nb, s, H_A * DH_A), BF16),
        scratch_shapes=[pltpu.VMEM((Q_BLOCK, s), I32)],
        compiler_params=_cparams(("parallel", "arbitrary")),
        name="dsa_prompt",
    )(r3(aq), r3(iq), r3(sm), r3(akb), r3(avb), r3(ik2))
    return out.reshape(nb * s, H_A * DH_A)


def _idx_lhs(iq_ref):
    iqf = iq_ref[0].astype(F32)
    return jnp.concatenate([iqf[:, h * D_IDX:(h + 1) * D_IDX] for h in range(H_I)], axis=0).astype(BF16)


def _idx_score(r, sm, t):
    sc = jnp.zeros((t, r.shape[1]), F32)
    for h in range(H_I):
        sc = sc + sm[:, SM_IW + h:SM_IW + h + 1] * jnp.maximum(r[h * t:(h + 1) * t, :], 0.0)
    return sc


def _dsa_s_score_kernel(pt_ref, iq_ref, sm_ref, ikn_ref, *rest, pg, t, topk, npast):
    pages = rest[:pg]
    skey_ref, skn_ref, thr_ref = rest[pg:]
    j = pl.program_id(1)
    lhs = _idx_lhs(iq_ref)
    sm = sm_ref[0]
    for r in range(pg):
        kp = pages[r][0].astype(BF16)
        sc = _idx_score(_dot_nt(lhs, kp), sm, t)
        off = pl.multiple_of((j * pg + r) * PAGE_SIZE, PAGE_SIZE)
        skey_ref[0, :, pl.ds(off, PAGE_SIZE)] = _okey(sc)

    @pl.when(j == pl.num_programs(1) - 1)
    def _():
        scn = _idx_score(_dot_nt(lhs, ikn_ref[0]), sm, t)
        ti = lax.broadcasted_iota(I32, (t, LANES), 0)
        si = lax.broadcasted_iota(I32, (t, LANES), 1)
        kn = _okey(jnp.where(si <= ti, scn, -jnp.inf))
        skn_ref[0] = kn
        cw = 1024 if npast % 1024 == 0 else PAGE_SIZE

        def count_ge(cand):
            def body(cidx, acc):
                k0 = pl.multiple_of(cidx * cw, cw)
                m = jnp.where(skey_ref[0, :, pl.ds(k0, cw)] >= cand, 1.0, 0.0)
                part = m[:, :LANES]
                for u in range(1, cw // LANES):
                    part = part + m[:, u * LANES:(u + 1) * LANES]
                return acc + part
            acc = lax.fori_loop(0, npast // cw, body, jnp.where(kn >= cand, 1.0, 0.0))
            return jnp.sum(acc, axis=1, keepdims=True)

        thr = _kth_largest_key(count_ge, topk, (t, 1))
        thr = jnp.maximum(thr, jnp.int32(NEG_INF_KEY + 1))
        thr_ref[0] = jnp.broadcast_to(thr, (t, LANES))


def _dsa_s_att_kernel(pt_ref, aq_ref, skey_ref, skn_ref, thr_ref, kn_ref, vn_ref, *rest, pg, t):
    kpages, vpages = rest[:pg], rest[pg:2 * pg]
    out_ref, m_sc, l_sc, acc_sc = rest[2 * pg:]
    j = pl.program_id(1)
    tp = 16

    @pl.when(j == 0)
    def _():
        m_sc[...] = jnp.full(m_sc.shape, NEG_BIG, F32)
        l_sc[...] = jnp.zeros(l_sc.shape, F32)
        acc_sc[...] = jnp.zeros(acc_sc.shape, F32)

    scale = DH_A ** -0.5
    aqf = aq_ref[0].astype(F32)
    aqp = jnp.concatenate([aqf, jnp.zeros((tp - t, aqf.shape[1]), F32)], axis=0).astype(BF16)
    thr = thr_ref[0][:, :1]

    def step(h, kb, vb, sel):
        sl = slice(h * LANES, (h + 1) * LANES)
        lg = jnp.where(sel, _dot_nt(aqp[:, sl], kb)[:t] * scale, NEG_BIG)
        m_i, l_i = m_sc[h][:, :1], l_sc[h][:, :1]
        m_new = jnp.maximum(m_i, jnp.max(lg, axis=1, keepdims=True))
        alpha = jnp.exp(m_i - m_new)
        p = jnp.where(sel, jnp.exp(lg - m_new), 0.0)
        l_new = alpha * l_i + jnp.sum(p, axis=1, keepdims=True)
        pp = jnp.concatenate([p, jnp.zeros((tp - t, p.shape[1]), F32)], axis=0).astype(BF16)
        acc = alpha * acc_sc[h] + _dot(pp, vb)[:t]
        m_sc[h] = jnp.broadcast_to(m_new, (t, LANES))
        l_sc[h] = jnp.broadcast_to(l_new, (t, LANES))
        acc_sc[h] = acc

    sel_past = skey_ref[0] >= thr
    for h in range(H_A):
        sl = slice(h * LANES, (h + 1) * LANES)
        kb = jnp.concatenate([kpages[r][0, :, sl].astype(BF16) for r in range(pg)], axis=0)
        vb = jnp.concatenate([vpages[r][0, :, sl].astype(BF16) for r in range(pg)], axis=0)
        step(h, kb, vb, sel_past)

    @pl.when(j == pl.num_programs(1) - 1)
    def _():
        sel_new = skn_ref[0] >= thr
        for h in range(H_A):
            sl = slice(h * LANES, (h + 1) * LANES)
            step(h, kn_ref[0, :, sl], vn_ref[0, :, sl], sel_new)
            out_ref[0, :, sl] = (acc_sc[h] / l_sc[h][:, :1]).astype(BF16)


def _dsa_sample(aq, iq, sm, ak, av, ik, cache_k, cache_v, cache_kidx, page_table, nb, t):
    n_pages = page_table.shape[1]
    npast = n_pages * PAGE_SIZE
    topk = min(TOPK_MAX, (npast + t) // 4)
    pg = math.gcd(8, n_pages)
    n_pool = cache_k.shape[0]
    ck = cache_k.reshape(n_pool, PAGE_SIZE, H_A * DH_A)
    cv = cache_v.reshape(n_pool, PAGE_SIZE, H_A * DH_A)
    r3 = lambda a: a.reshape(nb, t, a.shape[-1])
    padrows = lambda a: jnp.pad(r3(a), ((0, 0), (0, LANES - t), (0, 0))).astype(BF16)
    tokspec = lambda w: pl.BlockSpec((1, t, w), lambda b, j, pt: (b, 0, 0))
    newspec = lambda w: pl.BlockSpec((1, LANES, w), lambda b, j, pt: (b, 0, 0))

    def pagespec(w, r):
        return pl.BlockSpec((1, PAGE_SIZE, w), lambda b, j, pt, r=r: (pt[b, j * pg + r], 0, 0))

    skey, skn, thr = pl.pallas_call(
        functools.partial(_dsa_s_score_kernel, pg=pg, t=t, topk=topk, npast=npast),
        grid_spec=pltpu.PrefetchScalarGridSpec(
            num_scalar_prefetch=1,
            grid=(nb, n_pages // pg),
            in_specs=[tokspec(512), tokspec(LANES), newspec(D_IDX)] + [pagespec(D_IDX, r) for r in range(pg)],
            out_specs=[tokspec(npast), tokspec(LANES), tokspec(LANES)],
        ),
        out_shape=[jax.ShapeDtypeStruct((nb, t, npast), I32),
                   jax.ShapeDtypeStruct((nb, t, LANES), I32),
                   jax.ShapeDtypeStruct((nb, t, LANES), I32)],
        compiler_params=_cparams(("parallel", "arbitrary")),
        name="dsa_sample_score",
    )(page_table, r3(iq), r3(sm), padrows(ik), *([cache_kidx] * pg))

    att = pl.pallas_call(
        functools.partial(_dsa_s_att_kernel, pg=pg, t=t),
        grid_spec=pltpu.PrefetchScalarGridSpec(
            num_scalar_prefetch=1,
            grid=(nb, n_pages // pg),
            in_specs=[tokspec(512),
                      pl.BlockSpec((1, t, pg * PAGE_SIZE), lambda b, j, pt: (b, 0, j)),
                      tokspec(LANES), tokspec(LANES), newspec(512), newspec(512)]
                     + [pagespec(512, r) for r in range(pg)] + [pagespec(512, r) for r in range(pg)],
            out_specs=tokspec(512),
            scratch_shapes=[pltpu.VMEM((H_A, t, LANES), F32)] * 3,
        ),
        out_shape=jax.ShapeDtypeStruct((nb, t, H_A * DH_A), BF16),
        compiler_params=_cparams(("parallel", "arbitrary")),
        name="dsa_sample_att",
    )(page_table, r3(aq), skey, skn, thr, padrows(ak), padrows(av), *([ck] * pg), *([cv] * pg))
    return att.reshape(nb * t, H_A * DH_A)


def _merge_kernel(x_ref, hn_ref, att_ref, sg_ref, wa_ref, wb_ref, wo_ref, gf_ref, wpqt_ref, bd_ref,
                  h1_ref, xn2_ref, st_ref):
    sg = sg_ref[...]
    merged = sg[:, :D_MODEL] * _dot(hn_ref[...], wa_ref[...]) + sg[:, D_MODEL:] * _dot(att_ref[...], wb_ref[...])
    h1 = x_ref[...] + _dot(merged.astype(BF16), wo_ref[...])
    h1_ref[...] = h1
    xn2 = ((h1 * lax.rsqrt(jnp.mean(h1 * h1, axis=-1, keepdims=True) + EPS)) * gf_ref[...]).astype(BF16)
    xn2_ref[...] = xn2
    qt = _dot_nt(wpqt_ref[...], xn2).astype(BF16)
    for h in range(H_P):
        st_ref[h * 2 * N_KEYS:(h + 1) * 2 * N_KEYS, :] = _dot(bd_ref[h], qt[h * D_PKEY:(h + 1) * D_PKEY, :])


def _prep_subkeys(sub_keys):
    z = jnp.zeros_like(sub_keys[:, 0])
    top = jnp.concatenate([sub_keys[:, 0], z], axis=-1)
    bot = jnp.concatenate([z, sub_keys[:, 1]], axis=-1)
    return jnp.concatenate([top, bot], axis=1).astype(BF16)


def _merge(x2d, hn, att, sg, wa, wb, wo, g_ffn, wpqt, bd, tm):
    n = x2d.shape[0]
    assert n % tm == 0
    row = lambda w: pl.BlockSpec((tm, w), lambda i: (i, 0))
    full2 = lambda a: pl.BlockSpec(a.shape, lambda i: (0, 0))
    return pl.pallas_call(
        _merge_kernel,
        grid=(n // tm,),
        in_specs=[row(D_MODEL), row(512), row(512), row(2 * D_MODEL), full2(wa), full2(wb), full2(wo),
                  pl.BlockSpec((1, D_MODEL), lambda i: (0, 0)), full2(wpqt),
                  pl.BlockSpec(bd.shape, lambda i: (0, 0, 0))],
        out_specs=[row(D_MODEL), row(D_MODEL), pl.BlockSpec((H_P * 2 * N_KEYS, tm), lambda i: (0, i))],
        out_shape=[jax.ShapeDtypeStruct((n, D_MODEL), F32), jax.ShapeDtypeStruct((n, D_MODEL), BF16),
                   jax.ShapeDtypeStruct((H_P * 2 * N_KEYS, n), F32)],
        compiler_params=_cparams(("parallel",)),
        name="merge",
    )(x2d, hn, att, sg, wa, wb, wo, g_ffn.reshape(1, D_MODEL), wpqt, bd)


def _extract_top(s, k):
    rows = lax.broadcasted_iota(I32, s.shape, 0)
    cur, vals = s, []
    for _ in range(k):
        m = jnp.max(cur, axis=0, keepdims=True)
        first = jnp.min(jnp.where(cur == m, rows, s.shape[0]), axis=0, keepdims=True)
        cur = jnp.where(rows == first, -jnp.inf, cur)
        vals.append(m)
    return vals, cur


_STAIR = [(a, b) for a in range(P_TOPK) for b in range(P_TOPK) if (a + 1) * (b + 1) <= P_TOPK]


def _peer_prep_kernel(st_ref, s1m_ref, s2m_ref, e1_ref, e2_ref, tau_ref):
    tn = st_ref.shape[1]
    taus = []
    for h in range(H_P):
        s1 = st_ref[(2 * h) * N_KEYS:(2 * h + 1) * N_KEYS, :]
        s2 = st_ref[(2 * h + 1) * N_KEYS:(2 * h + 2) * N_KEYS, :]
        v1, r1 = _extract_top(s1, P_TOPK)
        v2, r2 = _extract_top(s2, P_TOPK)
        sel1, sel2 = r1 != s1, r2 != s2
        cand = [v1[a] + v2[b] for a, b in _STAIR]
        npad = (-len(cand)) % 8
        cand = jnp.concatenate(cand + [jnp.full((npad, tn), -jnp.inf, F32)], axis=0)
        top, _ = _extract_top(cand, P_TOPK)
        z = top[0] * 0.0
        for c in top:
            z = z + jnp.exp(c - top[0])
        taus.append(top[P_TOPK - 1])
        sl = slice(h * N_KEYS, (h + 1) * N_KEYS)
        s1m_ref[sl, :] = jnp.where(sel1, s1, -jnp.inf)
        s2m_ref[sl, :] = jnp.where(sel2, s2, -jnp.inf)
        e1_ref[sl, :] = jnp.where(sel1, jnp.exp(s1 - v1[0]), 0.0) / z
        e2_ref[sl, :] = jnp.where(sel2, jnp.exp(s2 - v2[0]), 0.0)
    tau_ref[...] = jnp.concatenate(taus, axis=0)


def _peer_prep(st, tn):
    n = st.shape[1]
    assert n % tn == 0
    col = lambda r: pl.BlockSpec((r, tn), lambda i: (0, i))
    hk = H_P * N_KEYS
    return pl.pallas_call(
        _peer_prep_kernel,
        grid=(n // tn,),
        in_specs=[col(2 * hk)],
        out_specs=[col(hk), col(hk), col(hk), col(hk), col(H_P)],
        out_shape=[jax.ShapeDtypeStruct((hk, n), F32)] * 4 + [jax.ShapeDtypeStruct((H_P, n), F32)],
        compiler_params=_cparams(("parallel",)),
        name="peer_prep",
    )(st)


def _peer_dense_kernel(u_ref, vt_ref, xn2_ref, s1m_ref, s2m_ref, e1_ref, e2_ref, tau_ref, h1_ref, gf_ref,
                       y_ref, acc_ref):
    e = pl.program_id(1)
    te = u_ref.shape[0]
    nblk = te // N_KEYS

    @pl.when(e == 0)
    def _():
        acc_ref[...] = jnp.zeros(acc_ref.shape, F32)

    a_t = _dot_nt(u_ref[...], xn2_ref[...])
    gl = 0.5 * a_t * (1.0 + lax.erf(a_t * (2.0 ** -0.5)))
    ys = []
    for blk in range(nblk):
        a_idx = e * nblk + blk
        w = jnp.zeros((N_KEYS, a_t.shape[1]), F32)
        for h in range(H_P):
            sl = slice(h * N_KEYS, (h + 1) * N_KEYS)
            s1row = s1m_ref[pl.ds(h * N_KEYS + a_idx, 1), :]
            e1row = e1_ref[pl.ds(h * N_KEYS + a_idx, 1), :]
            t = s2m_ref[sl, :] + s1row
            w = w + jnp.where(t >= tau_ref[h:h + 1, :], e2_ref[sl, :], 0.0) * e1row
        ys.append((w * gl[blk * N_KEYS:(blk + 1) * N_KEYS, :]).astype(BF16))
    y = jnp.concatenate(ys, axis=0) if nblk > 1 else ys[0]
    acc_ref[...] += _dot(vt_ref[...], y)

    @pl.when(e == pl.num_programs(1) - 1)
    def _():
        h2 = h1_ref[...] + acc_ref[...].T
        y_ref[...] = (h2 * lax.rsqrt(jnp.mean(h2 * h2, axis=-1, keepdims=True) + EPS)) * gf_ref[...]


def _peer_dense(ub, vtb, xn2, s1m, s2m, e1, e2, tau, h1, g_final, tn, te):
    n = xn2.shape[0]
    ne = ub.shape[0]
    assert n % tn == 0 and ne % te == 0
    hk = H_P * N_KEYS
    col = lambda r: pl.BlockSpec((r, tn), lambda i, e: (0, i))
    row = lambda w: pl.BlockSpec((tn, w), lambda i, e: (i, 0))
    return pl.pallas_call(
        _peer_dense_kernel,
        grid=(n // tn, ne // te),
        in_specs=[pl.BlockSpec((te, D_MODEL), lambda i, e: (e, 0)),
                  pl.BlockSpec((D_MODEL, te), lambda i, e: (0, e)),
                  row(D_MODEL), col(hk), col(hk), col(hk), col(hk), col(H_P), row(D_MODEL),
                  pl.BlockSpec((1, D_MODEL), lambda i, e: (0, 0))],
        out_specs=row(D_MODEL),
        out_shape=jax.ShapeDtypeStruct((n, D_MODEL), F32),
        scratch_shapes=[pltpu.VMEM((D_MODEL, tn), F32)],
        compiler_params=_cparams(("parallel", "arbitrary")),
        name="peer_dense",
    )(ub, vtb, xn2, s1m, s2m, e1, e2, tau, h1, g_final.reshape(1, D_MODEL))


def _row_tile(n, pref):
    t = pref
    while n % t:
        t //= 2
    return t


def _group(x, pos, tables_rows, attend, c0, n0, m0, lw, fw, g_final):
    nb, t, _ = x.shape
    n = nb * t
    g_mix, wp, bias_row, g_mnorm, wa, wb, wo = lw
    g_ffn, wpqt, bd, ub, vtb = fw
    x2d = x.reshape(n, D_MODEL)
    tm = _row_tile(n, 256)
    tables = _rope_tables(pos)
    if tables_rows != t:
        tables = tuple(jnp.tile(a, (tables_rows // t, 1)) for a in tables)
    (mq, mk, mv, mo, aq, ak, av, akb, avb, iq, sg, sm, ik, ik2) = _inproj(x2d, g_mix, wp, bias_row, tables, tm)
    hn, cc, nn, mm = _mlstm(mq, mk, mv, sm, mo, g_mnorm, c0, n0, m0, nb, t)
    att = attend(aq, iq, sm, ak, av, akb, avb, ik, ik2)
    h1, xn2, st = _merge(x2d, hn, att, sg, wa, wb, wo, g_ffn, wpqt, bd, tm)
    tn = _row_tile(n, 512)
    s1m, s2m, e1, e2, tau = _peer_prep(st, _row_tile(n, 256))
    y = _peer_dense(ub, vtb, xn2, s1m, s2m, e1, e2, tau, h1, g_final, tn, 256)
    return (y.reshape(nb, t, D_MODEL), ak.reshape(nb, t, H_A, DH_A), av.reshape(nb, t, H_A, DH_A),
            ik.reshape(nb, t, D_IDX), cc, nn, mm)


def kernel(x_prompt, x_sample, cache_k, cache_v, cache_kidx, page_table, state_C, state_n, state_m,
           g_mix, w_in, b_mgate, g_mnorm, w_a, w_b, w_o, g_ffn, w_pq, sub_keys, peer_u, peer_v, g_final):
    depth = w_in.shape[0]
    assert depth == 1, "single-layer step"
    l = 0
    bp, sp, _ = x_prompt.shape
    bd_, td, _ = x_sample.shape
    past = page_table.shape[1] * PAGE_SIZE

    bias_row = jnp.zeros((1, LANES), F32).at[0, SM_IP:SM_IW].set(b_mgate[l].astype(F32))
    lw = (g_mix[l], _prep_w_in(w_in[l]), bias_row, g_mnorm[l],
          w_a[l].astype(BF16), w_b[l].astype(BF16), w_o[l].astype(BF16))
    fw = (g_ffn[l], w_pq[l].T.astype(BF16), _prep_subkeys(sub_keys[l]),
          peer_u[l].astype(BF16), peer_v[l].T.astype(BF16))

    def attend_p(aq, iq, sm, ak, av, akb, avb, ik, ik2):
        return _dsa_prompt(aq, iq, sm, akb, avb, ik2, bp, sp)

    def attend_s(aq, iq, sm, ak, av, akb, avb, ik, ik2):
        return _dsa_sample(aq, iq, sm, ak, av, ik, cache_k[l], cache_v[l], cache_kidx[l], page_table, bd_, td)

    zc = jnp.zeros((bp, H_M, DV_M, DK_M), F32)
    zn = jnp.zeros((bp, H_M, DK_M), F32)
    zm = jnp.zeros((bp, H_M), F32)
    yp, kp, vp, kip, cp, np_, mp = _group(x_prompt, jnp.arange(sp), sp, attend_p, zc, zn, zm, lw, fw, g_final)
    ys, ks, vs, kis, cs, ns, ms = _group(x_sample, past + jnp.arange(td), bd_ * td, attend_s,
                                         state_C[l], state_n[l], state_m[l], lw, fw, g_final)
    st = lambda a: a[None]
    return (yp, ys, st(kp), st(vp), st(kip), st(cp), st(np_), st(mp),
            st(ks), st(vs), st(kis), st(cs), st(ns), st(ms))
```

```python
import functools
import math

import jax
import jax.numpy as jnp
from jax import lax
from jax.experimental import pallas as pl
from jax.experimental.pallas import tpu as pltpu

F32 = jnp.float32
BF16 = jnp.bfloat16
I32 = jnp.int32

D_MODEL = 1024
PAGE_SIZE = 128
H_M, DK_M, DV_M, M_CHUNK = 4, 128, 128, 64
H_A, DH_A = 4, 128
H_I, D_IDX = 8, 64
TOPK_MAX = 256
Q_BLOCK = 128
ROPE_THETA = 10000.0
N_KEYS = 128
H_P, D_PKEY, P_TOPK = 8, 128, 16
EPS = 1e-6

MIX_SPLITS = (H_M * DK_M, H_M * DK_M, H_M * DV_M, H_M, H_M, H_M * DV_M,
              H_A * DH_A, H_A * DH_A, H_A * DH_A, H_I * D_IDX, D_IDX, H_I, 2 * D_MODEL)

LANES = 128
VMEM_LIMIT_BYTES = 56 * 1024 * 1024
NEG_INF_KEY = -2139095041
INT_MIN = -2147483648
NEG_BIG = -1e30

SM_IK, SM_IP, SM_LF, SM_IW, SM_END = 0, 64, 68, 72, 80
W_MQ, W_MK, W_MV, W_MO, W_AQ, W_AK, W_AV, W_IQ, W_G, W_SM, W_COLS = (
    0, 512, 1024, 1536, 2048, 2560, 3072, 3584, 4096, 6144, 6272)


def _dot(a, b):
    return jnp.dot(a, b, preferred_element_type=F32)


def _dot_nt(a, b):
    return lax.dot_general(a, b, (((1,), (1,)), ((), ())), preferred_element_type=F32)


def _dot_tn(a, b):
    return lax.dot_general(a, b, (((0,), (0,)), ((), ())), preferred_element_type=F32)


def _okey(x):
    i = lax.bitcast_convert_type(x, I32)
    return i ^ ((i >> 31) & jnp.int32(0x7FFFFFFF))


def _cparams(sem):
    return pltpu.CompilerParams(dimension_semantics=sem, vmem_limit_bytes=VMEM_LIMIT_BYTES)


def _inproj_kernel(x_ref, g_ref, w_ref, bias_ref, cosa_ref, sina_ref, cosi_ref, sini_ref,
                   mq_ref, mk_ref, mv_ref, mo_ref, aq_ref, ak_ref, av_ref, akb_ref, avb_ref,
                   iq_ref, sg_ref, sm_ref, ik_ref, ik2_ref):
    x = x_ref[...]
    ms = jnp.mean(x * x, axis=-1, keepdims=True)
    xn = ((x * lax.rsqrt(ms + EPS)) * g_ref[...]).astype(BF16)

    def proj(lo, width):
        return _dot(xn, w_ref[:, lo:lo + width])

    cosa, sina = cosa_ref[...], sina_ref[...]
    cosi, sini = cosi_ref[...], sini_ref[...]
    lane = lax.broadcasted_iota(I32, (x.shape[0], LANES), 1)
    first_half = (lane % 64) < 32

    def rope128(z):
        outs = []
        for h in range(z.shape[1] // LANES):
            zh = z[:, h * LANES:(h + 1) * LANES]
            outs.append(zh * cosa + pltpu.roll(zh, 64, 1) * sina)
        return jnp.concatenate(outs, axis=1) if len(outs) > 1 else outs[0]

    def rope64_slab(zh):
        rot = jnp.where(first_half, pltpu.roll(zh, 96, 1), pltpu.roll(zh, 32, 1))
        return zh * cosi + rot * sini

    mq_ref[...] = (proj(W_MQ, 512) * (DK_M ** -0.5)).astype(BF16)
    mk_ref[...] = proj(W_MK, 512).astype(BF16)
    mv_ref[...] = proj(W_MV, 512).astype(BF16)
    mo_ref[...] = proj(W_MO, 512)
    aq_ref[...] = rope128(proj(W_AQ, 512)).astype(BF16)
    ak = rope128(proj(W_AK, 512))
    ak_ref[...] = ak
    akb_ref[...] = ak.astype(BF16)
    av = proj(W_AV, 512)
    av_ref[...] = av
    avb_ref[...] = av.astype(BF16)
    ziq = proj(W_IQ, 512)
    iq = jnp.concatenate([rope64_slab(ziq[:, h * LANES:(h + 1) * LANES]) for h in range(4)], axis=1)
    iq_ref[...] = (iq * (D_IDX ** -0.5)).astype(BF16)
    sg_ref[...] = jax.nn.sigmoid(proj(W_G, 2 * D_MODEL))
    zs = proj(W_SM, LANES)
    ikr = rope64_slab(zs)
    zb = zs + bias_ref[...]
    logsig = jnp.minimum(zb, 0.0) - jnp.log1p(jnp.exp(-jnp.abs(zb)))
    sm = jnp.where(lane < SM_IP, ikr,
                   jnp.where(lane < SM_LF, zb,
                             jnp.where(lane < SM_IW, logsig,
                                       jnp.where(lane < SM_END, zs * (H_I ** -0.5), 0.0))))
    sm_ref[...] = sm
    ik_ref[...] = ikr[:, :D_IDX]
    ik2_ref[...] = jnp.where(lane < D_IDX, ikr, pltpu.roll(ikr, 64, 1)).astype(BF16)


def _prep_w_in(w_in):
    parts, o = [], 0
    for s in MIX_SPLITS:
        parts.append(w_in[:, o:o + s])
        o += s
    mq, mk, mv, mi, mf, mo, aq, ak, av, iq, ik, iw, gates = parts
    pad = jnp.zeros((w_in.shape[0], LANES - SM_END), w_in.dtype)
    small = jnp.concatenate([ik, mi, mf, iw, pad], axis=1)
    return jnp.concatenate([mq, mk, mv, mo, aq, ak, av, iq, gates, small], axis=1).astype(BF16)


def _rope_tables(pos):
    pos = pos.astype(F32)[:, None]

    def tab(d):
        inv = ROPE_THETA ** (-jnp.arange(0, d, 2, dtype=F32) / d)
        ang = pos * inv[None, :]
        c, s = jnp.cos(ang), jnp.sin(ang)
        cos = jnp.concatenate([c, c], axis=1)
        sin = jnp.concatenate([-s, s], axis=1)
        reps = LANES // d
        return jnp.tile(cos, (1, reps)), jnp.tile(sin, (1, reps))

    cosa, sina = tab(DH_A)
    cosi, sini = tab(D_IDX)
    return cosa, sina, cosi, sini


def _inproj(x2d, g_mix, wp, bias_row, tables, tm):
    n = x2d.shape[0]
    t_tab = tables[0].shape[0]
    assert n % tm == 0 and t_tab % tm == 0
    nt = t_tab // tm
    row = lambda i: (i, 0)
    tabspec = pl.BlockSpec((tm, LANES), lambda i: (i % nt, 0))
    full = lambda shape: pl.BlockSpec(shape, lambda i: (0, 0))
    out_defs = [
        (512, BF16), (512, BF16), (512, BF16), (512, F32), (512, BF16), (512, F32), (512, F32),
        (512, BF16), (512, BF16), (512, BF16), (2 * D_MODEL, F32), (LANES, F32), (D_IDX, F32), (LANES, BF16)]
    return pl.pallas_call(
        _inproj_kernel,
        grid=(n // tm,),
        in_specs=[pl.BlockSpec((tm, D_MODEL), row), full((1, D_MODEL)), full((D_MODEL, W_COLS)),
                  full((1, LANES)), tabspec, tabspec, tabspec, tabspec],
        out_specs=[pl.BlockSpec((tm, w), row) for w, _ in out_defs],
        out_shape=[jax.ShapeDtypeStruct((n, w), dt) for w, dt in out_defs],
        compiler_params=_cparams(("parallel",)),
        name="inproj",
    )(x2d, g_mix.reshape(1, D_MODEL), wp, bias_row, *tables)


def _mlstm_kernel(q_ref, k_ref, v_ref, sm_ref, mo_ref, gm_ref, c0_ref, n0_ref, m0_ref,
                  hn_ref, c_ref, n_ref, m_ref, *, c, nchunks):
    @pl.when(pl.program_id(1) == 0)
    def _():
        c_ref[...] = c0_ref[...]
        n_ref[...] = n0_ref[...]
        m_ref[...] = m0_ref[...]

    ri = lax.broadcasted_iota(I32, (c, c), 0)
    ci = lax.broadcasted_iota(I32, (c, c), 1)
    eye = ri == ci
    tri = ri >= ci

    def to_row(col):
        return jnp.sum(jnp.where(eye, jnp.broadcast_to(col, (c, c)), 0.0), axis=0, keepdims=True)

    def chunk(rows):
        q = q_ref[0, rows, :]
        k = k_ref[0, rows, :]
        v = v_ref[0, rows, :]
        sm = sm_ref[0, rows, :]
        mo = mo_ref[0, rows, :]
        for h in range(H_M):
            sl = slice(h * LANES, (h + 1) * LANES)
            qh, kh, vh = q[:, sl], k[:, sl], v[:, sl]
            ipc = sm[:, SM_IP + h:SM_IP + h + 1]
            lfc = sm[:, SM_LF + h:SM_LF + h + 1]
            lf_row = to_row(lfc)
            ip_row = to_row(ipc)
            b_col = jnp.sum(jnp.where(tri, jnp.broadcast_to(lf_row, (c, c)), 0.0), axis=1, keepdims=True)
            b_row = to_row(b_col)
            m_prev = m_ref[0, h][:, :1]
            n_prev = n_ref[0, h]
            c_prev = c_ref[0, h]
            m_inter = b_col + m_prev
            dmat = jnp.where(tri, b_col - b_row + ip_row, -jnp.inf)
            m_t = jnp.maximum(m_inter, jnp.max(dmat, axis=1, keepdims=True))
            w_inter = jnp.exp(m_inter - m_t)
            s = _dot_nt(qh, kh) * jnp.exp(dmat - m_t)
            num = w_inter * _dot_nt(qh, c_prev.astype(BF16)) + _dot(s.astype(BF16), vh)
            den = (w_inter * jnp.sum(qh.astype(F32) * n_prev, axis=1, keepdims=True)
                   + jnp.sum(s, axis=1, keepdims=True))
            hh = num / jnp.maximum(jnp.abs(den), jnp.exp(-m_t))
            m_new = m_t[c - 1:c, :]
            b_last = b_col[c - 1:c, :]
            g_state = jnp.exp(b_last + m_prev - m_new)
            g_tok = jnp.exp(b_last - b_col + ipc - m_new)
            gv = (g_tok * vh.astype(F32)).astype(BF16)
            c_ref[0, h] = g_state * c_prev + _dot_tn(gv, kh)
            n_ref[0, h] = g_state * n_prev + jnp.sum(g_tok * kh.astype(F32), axis=0, keepdims=True)
            m_ref[0, h] = jnp.broadcast_to(m_new, (1, LANES))
            y = hh * lax.rsqrt(jnp.mean(hh * hh, axis=1, keepdims=True) + EPS) * gm_ref[:, sl]
            hn_ref[0, rows, sl] = (jax.nn.sigmoid(mo[:, sl]) * y).astype(BF16)

    if nchunks == 1:
        chunk(slice(0, c))
    else:
        def body(i, carry):
            chunk(pl.ds(pl.multiple_of(i * c, c), c))
            return carry
        lax.fori_loop(0, nchunks, body, 0)


def _mlstm(mq, mk, mv, sm, mo, g_mnorm, c0, n0, m0, nb, t):
    c = math.gcd(M_CHUNK, t)
    tb = min(t, 8 * c)
    assert t % tb == 0
    r3 = lambda a: a.reshape(nb, t, a.shape[-1])
    n0b = n0.reshape(nb, H_M, 1, DK_M).astype(F32)
    m0b = jnp.broadcast_to(m0.astype(F32)[:, :, None, None], (nb, H_M, 1, LANES))
    tok = lambda w: pl.BlockSpec((1, tb, w), lambda b, j: (b, j, 0))
    st4 = lambda shape: pl.BlockSpec(shape, lambda b, j: (b, 0, 0, 0))
    hn, cc, nn, mm = pl.pallas_call(
        functools.partial(_mlstm_kernel, c=c, nchunks=tb // c),
        grid=(nb, t // tb),
        in_specs=[tok(512), tok(512), tok(512), tok(LANES), tok(512),
                  pl.BlockSpec((1, H_M * DV_M), lambda b, j: (0, 0)),
                  st4((1, H_M, DV_M, DK_M)), st4((1, H_M, 1, DK_M)), st4((1, H_M, 1, LANES))],
        out_specs=[tok(512), st4((1, H_M, DV_M, DK_M)), st4((1, H_M, 1, DK_M)), st4((1, H_M, 1, LANES))],
        out_shape=[jax.ShapeDtypeStruct((nb, t, H_M * DV_M), BF16),
                   jax.ShapeDtypeStruct((nb, H_M, DV_M, DK_M), F32),
                   jax.ShapeDtypeStruct((nb, H_M, 1, DK_M), F32),
                   jax.ShapeDtypeStruct((nb, H_M, 1, LANES), F32)],
        compiler_params=_cparams(("parallel", "arbitrary")),
        name="mlstm",
    )(r3(mq), r3(mk), r3(mv), r3(sm), r3(mo), g_mnorm.reshape(1, H_M * DV_M),
      c0.astype(F32), n0b, m0b)
    return hn.reshape(nb * t, H_M * DV_M), cc, nn.reshape(nb, H_M, DK_M), mm[:, :, 0, 0]


def _kth_largest_key(count_ge, k, shape):
    kf = jnp.float32(k)
    r0 = jnp.where(count_ge(jnp.zeros(shape, I32)) >= kf, jnp.int32(0), jnp.int32(INT_MIN))

    def body(t, r):
        cand = r + lax.shift_left(jnp.int32(1), jnp.int32(30) - t)
        return jnp.where(count_ge(cand) >= kf, cand, r)

    return lax.fori_loop(0, 31, body, r0)


def _dsa_prompt_kernel(aq_ref, iq_ref, sm_ref, kb_ref, vb_ref, ik2_ref, out_ref, skey_ref, acc_ref, *, kc, topk):
    i = pl.program_id(1)
    qb = Q_BLOCK
    nk = (i * qb + qb + kc - 1) // kc
    iq = iq_ref[0]
    lane = lax.broadcasted_iota(I32, (qb, LANES), 1)
    zero = jnp.zeros((qb, LANES), BF16)
    lhs = []
    for hp in range(H_I // 2):
        slab = iq[:, hp * LANES:(hp + 1) * LANES]
        lhs.append(jnp.where(lane < D_IDX, slab, zero))
        lhs.append(jnp.where(lane >= D_IDX, slab, zero))
    sm = sm_ref[0]
    qpos = i * qb + lax.broadcasted_iota(I32, (qb, 1), 0)

    def score_body(cidx, carry):
        k0 = pl.multiple_of(cidx * kc, kc)
        kk = ik2_ref[0, pl.ds(k0, kc), :]
        sc = jnp.zeros((qb, kc), F32)
        for h in range(H_I):
            sc = sc + sm[:, SM_IW + h:SM_IW + h + 1] * jnp.maximum(_dot_nt(lhs[h], kk), 0.0)
        kpos = k0 + lax.broadcasted_iota(I32, (1, kc), 1)
        sc = jnp.where(kpos <= qpos, sc, -jnp.inf)
        skey_ref[:, pl.ds(k0, kc)] = _okey(sc)
        return carry

    lax.fori_loop(0, nk, score_body, 0)

    def count_ge(cand):
        def body(cidx, acc):
            k0 = pl.multiple_of(cidx * kc, kc)
            ks = skey_ref[:, pl.ds(k0, kc)]
            m = jnp.where(ks >= cand, 1.0, 0.0)
            part = m[:, :LANES]
            for t in range(1, kc // LANES):
                part = part + m[:, t * LANES:(t + 1) * LANES]
            return acc + part
        acc = lax.fori_loop(0, nk, body, jnp.zeros((qb, LANES), F32))
        return jnp.sum(acc, axis=1, keepdims=True)

    thr = _kth_largest_key(count_ge, topk, (qb, 1))
    thr = jnp.maximum(thr, jnp.int32(NEG_INF_KEY + 1))
    scale = DH_A ** -0.5
    aq = aq_ref[0]
    acc_ref[...] = jnp.zeros(acc_ref.shape, F32)

    def att_body(cidx, carry):
        k0 = pl.multiple_of(cidx * kc, kc)
        sel = skey_ref[:, pl.ds(k0, kc)] >= thr
        out = []
        for h in range(H_A):
            sl = slice(h * LANES, (h + 1) * LANES)
            m_i, l_i = carry[2 * h], carry[2 * h + 1]
            lg = jnp.where(sel, _dot_nt(aq[:, sl], kb_ref[0, pl.ds(k0, kc), sl]) * scale, NEG_BIG)
            m_new = jnp.maximum(m_i, jnp.max(lg, axis=1, keepdims=True))
            alpha = jnp.exp(m_i - m_new)
            p = jnp.where(sel, jnp.exp(lg - m_new), 0.0)
            out += [m_new, alpha * l_i + jnp.sum(p, axis=1, keepdims=True)]
            acc_ref[h] = alpha * acc_ref[h] + _dot(p.astype(BF16), vb_ref[0, pl.ds(k0, kc), sl])
        return tuple(out)

    init = (jnp.full((qb, 1), NEG_BIG, F32), jnp.zeros((qb, 1), F32)) * H_A
    fin = lax.fori_loop(0, nk, att_body, init)
    for h in range(H_A):
        out_ref[0, :, h * LANES:(h + 1) * LANES] = (acc_ref[h] / fin[2 * h + 1]).astype(BF16)


def _dsa_prompt(aq, iq, sm, akb, avb, ik2, nb, s):
    topk = min(TOPK_MAX, s // 4)
    kc = min(512, s)
    assert s % kc == 0 and s % Q_BLOCK == 0 and kc % Q_BLOCK == 0
    r3 = lambda a: a.reshape(nb, s, a.shape[-1])
    qspec = lambda w: pl.BlockSpec((1, Q_BLOCK, w), lambda b, i: (b, i, 0))
    kspec = lambda w: pl.BlockSpec((1, s, w), lambda b, i: (b, 0, 0))
    out = pl.pallas_call(
        functools.partial(_dsa_prompt_kernel, kc=kc, topk=topk),
        grid=(nb, s // Q_BLOCK),
        in_specs=[qspec(512), qspec(512), qspec(LANES), kspec(512), kspec(512), kspec(LANES)],
        out_specs=qspec(512),
        out_shape=jax.ShapeDtypeStruct((nb, s, H_A * DH_A), BF16),
        scratch_shapes=[pltpu.VMEM((Q_BLOCK, s), I32), pltpu.VMEM((H_A, Q_BLOCK, DH_A), F32)],
        compiler_params=_cparams(("parallel", "arbitrary")),
        name="dsa_prompt",
    )(r3(aq), r3(iq), r3(sm), r3(akb), r3(avb), r3(ik2))
    return out.reshape(nb * s, H_A * DH_A)


def _idx_lhs_pair(iq_ref):
    iqf = iq_ref[0].astype(F32)
    lane = lax.broadcasted_iota(I32, (iqf.shape[0], LANES), 1)
    lo, hi = [], []
    for hp in range(H_I // 2):
        slab = iqf[:, hp * LANES:(hp + 1) * LANES]
        even_lo = jnp.where(lane < D_IDX, slab, 0.0)
        odd_hi = jnp.where(lane >= D_IDX, slab, 0.0)
        lo += [even_lo, pltpu.roll(odd_hi, D_IDX, 1)]
        hi += [pltpu.roll(even_lo, D_IDX, 1), odd_hi]
    return jnp.concatenate(lo, axis=0).astype(BF16), jnp.concatenate(hi, axis=0).astype(BF16)


def _idx_score(r, sm, t):
    sc = jnp.zeros((t, r.shape[1]), F32)
    for h in range(H_I):
        sc = sc + sm[:, SM_IW + h:SM_IW + h + 1] * jnp.maximum(r[h * t:(h + 1) * t, :], 0.0)
    return sc


def _dsa_s_score_kernel(pt_ref, iq_ref, sm_ref, ikn_ref, *rest, pgs, pg, t, topk, npast):
    pages = rest[:pgs]
    skey_ref, skn_ref, thr_ref = rest[pgs:]
    j = pl.program_id(1)
    lhs_lo, lhs_hi = _idx_lhs_pair(iq_ref)
    sm = sm_ref[0]
    half = pg * PAGE_SIZE // 2
    pv = jnp.concatenate([pages[r][0] for r in range(pgs)], axis=0).astype(BF16)
    k_lo = _okey(_idx_score(_dot_nt(lhs_lo, pv), sm, t))
    k_hi = _okey(_idx_score(_dot_nt(lhs_hi, pv), sm, t))
    base = pl.multiple_of(j * (pgs * PAGE_SIZE), pgs * PAGE_SIZE)
    for g in range(pgs // pg):
        skey_ref[0, :, pl.ds(base + 2 * g * half, half)] = k_lo[:, g * half:(g + 1) * half]
        skey_ref[0, :, pl.ds(base + (2 * g + 1) * half, half)] = k_hi[:, g * half:(g + 1) * half]

    @pl.when(j == pl.num_programs(1) - 1)
    def _():
        scn = _idx_score(_dot_nt(lhs_lo[:, :D_IDX], ikn_ref[0]), sm, t)
        ti = lax.broadcasted_iota(I32, (t, LANES), 0)
        si = lax.broadcasted_iota(I32, (t, LANES), 1)
        kn = _okey(jnp.where(si <= ti, scn, -jnp.inf))
        skn_ref[0] = kn
        cw = 1024 if npast % 1024 == 0 else 2 * half

        def count_ge(cand):
            def body(cidx, acc):
                k0 = pl.multiple_of(cidx * cw, cw)
                m = jnp.where(skey_ref[0, :, pl.ds(k0, cw)] >= cand, 1.0, 0.0)
                part = m[:, :LANES]
                for u in range(1, cw // LANES):
                    part = part + m[:, u * LANES:(u + 1) * LANES]
                return acc + part
            acc = lax.fori_loop(0, npast // cw, body, jnp.where(kn >= cand, 1.0, 0.0))
            return jnp.sum(acc, axis=1, keepdims=True)

        thr = _kth_largest_key(count_ge, topk, (t, 1))
        thr = jnp.maximum(thr, jnp.int32(NEG_INF_KEY + 1))
        thr_ref[0] = jnp.broadcast_to(thr, (t, LANES))


def _dsa_s_att_kernel(pt_ref, aq_ref, skey_ref, skn_ref, thr_ref, kn_ref, vn_ref, *rest, pg, t):
    kpages, vpages = rest[:pg], rest[pg:2 * pg]
    out_ref, m_sc, l_sc, acc_sc = rest[2 * pg:]
    j = pl.program_id(1)
    hw = H_A * DH_A

    @pl.when(j == 0)
    def _():
        m_sc[...] = jnp.full(m_sc.shape, NEG_BIG, F32)
        l_sc[...] = jnp.zeros(l_sc.shape, F32)
        acc_sc[...] = jnp.zeros(acc_sc.shape, F32)

    scale = DH_A ** -0.5
    aqf = aq_ref[0].astype(F32)
    qs = jnp.concatenate([aqf[:, h * LANES:(h + 1) * LANES] for h in range(H_A)], axis=0).astype(BF16)
    thr = thr_ref[0][:, :1]
    rows = lambda a, h: a[h * t:(h + 1) * t]

    def step(kbs, vbs, keys):
        sel1 = jnp.where(keys >= thr, 1, 0)
        sel = jnp.concatenate([sel1] * H_A, axis=0) > 0
        lg = jnp.concatenate([rows(_dot_nt(qs, kbs[h]), h) for h in range(H_A)], axis=0) * scale
        lg = jnp.where(sel, lg, NEG_BIG)
        m_i, l_i = m_sc[:, :1], l_sc[:, :1]
        m_new = jnp.maximum(m_i, jnp.max(lg, axis=1, keepdims=True))
        alpha = jnp.exp(m_i - m_new)
        p = jnp.where(sel, jnp.exp(lg - m_new), 0.0)
        l_new = alpha * l_i + jnp.sum(p, axis=1, keepdims=True)
        pb = p.astype(BF16)
        pv = jnp.concatenate([rows(_dot(pb, vbs[h]), h) for h in range(H_A)], axis=0)
        acc_sc[...] = alpha * acc_sc[...] + pv
        m_sc[...] = jnp.broadcast_to(m_new, m_sc.shape)
        l_sc[...] = jnp.broadcast_to(l_new, l_sc.shape)

    def head_rows(pages, h):
        ev = [pages[r][0, :, h * LANES:(h + 1) * LANES] for r in range(pg)]
        od = [pages[r][0, :, hw + h * LANES:hw + (h + 1) * LANES] for r in range(pg)]
        return jnp.concatenate(ev + od, axis=0).astype(BF16)

    step([head_rows(kpages, h) for h in range(H_A)], [head_rows(vpages, h) for h in range(H_A)], skey_ref[0])

    @pl.when(j == pl.num_programs(1) - 1)
    def _():
        step([kn_ref[0, :, h * LANES:(h + 1) * LANES] for h in range(H_A)],
             [vn_ref[0, :, h * LANES:(h + 1) * LANES] for h in range(H_A)], skn_ref[0])
        o = acc_sc[...] / l_sc[:, :1]
        for h in range(H_A):
            out_ref[0, :, h * LANES:(h + 1) * LANES] = rows(o, h).astype(BF16)


def _dsa_sample(aq, iq, sm, ak, av, ik, cache_k, cache_v, cache_kidx, page_table, nb, t):
    n_pages = page_table.shape[1]
    npast = n_pages * PAGE_SIZE
    topk = min(TOPK_MAX, (npast + t) // 4)
    pg = math.gcd(8, n_pages)
    assert pg % 2 == 0
    n_pool = cache_k.shape[0]
    hp = PAGE_SIZE // 2
    ck = cache_k.reshape(n_pool, hp, 2 * H_A * DH_A)
    cv = cache_v.reshape(n_pool, hp, 2 * H_A * DH_A)
    cx = cache_kidx.reshape(n_pool, hp, 2 * D_IDX)
    r3 = lambda a: a.reshape(nb, t, a.shape[-1])
    padrows = lambda a: jnp.pad(r3(a), ((0, 0), (0, LANES - t), (0, 0))).astype(BF16)
    tokspec = lambda w: pl.BlockSpec((1, t, w), lambda b, j, pt: (b, 0, 0))
    newspec = lambda w: pl.BlockSpec((1, LANES, w), lambda b, j, pt: (b, 0, 0))

    pgs = math.gcd(32, n_pages)

    def pagespec(w, r, per_step=pg):
        return pl.BlockSpec((1, hp, w), lambda b, j, pt, r=r: (pt[b, j * per_step + r], 0, 0))

    skey, skn, thr = pl.pallas_call(
        functools.partial(_dsa_s_score_kernel, pgs=pgs, pg=pg, t=t, topk=topk, npast=npast),
        grid_spec=pltpu.PrefetchScalarGridSpec(
            num_scalar_prefetch=1,
            grid=(nb, n_pages // pgs),
            in_specs=[tokspec(512), tokspec(LANES), newspec(D_IDX)]
                     + [pagespec(2 * D_IDX, r, pgs) for r in range(pgs)],
            out_specs=[tokspec(npast), tokspec(LANES), tokspec(LANES)],
        ),
        out_shape=[jax.ShapeDtypeStruct((nb, t, npast), I32),
                   jax.ShapeDtypeStruct((nb, t, LANES), I32),
                   jax.ShapeDtypeStruct((nb, t, LANES), I32)],
        compiler_params=_cparams(("parallel", "arbitrary")),
        name="dsa_sample_score",
    )(page_table, r3(iq), r3(sm), padrows(ik), *([cx] * pgs))

    att = pl.pallas_call(
        functools.partial(_dsa_s_att_kernel, pg=pg, t=t),
        grid_spec=pltpu.PrefetchScalarGridSpec(
            num_scalar_prefetch=1,
            grid=(nb, n_pages // pg),
            in_specs=[tokspec(512),
                      pl.BlockSpec((1, t, pg * PAGE_SIZE), lambda b, j, pt: (b, 0, j)),
                      tokspec(LANES), tokspec(LANES), newspec(512), newspec(512)]
                     + [pagespec(2 * H_A * DH_A, r) for r in range(pg)] * 2,
            out_specs=tokspec(512),
            scratch_shapes=[pltpu.VMEM((H_A * t, LANES), F32)] * 3,
        ),
        out_shape=jax.ShapeDtypeStruct((nb, t, H_A * DH_A), BF16),
        compiler_params=_cparams(("parallel", "arbitrary")),
        name="dsa_sample_att",
    )(page_table, r3(aq), skey, skn, thr, padrows(ak), padrows(av), *([ck] * pg), *([cv] * pg))
    return att.reshape(nb * t, H_A * DH_A)


def _merge_kernel(x_ref, hn_ref, att_ref, sg_ref, wa_ref, wb_ref, wo_ref, gf_ref, wpqt_ref, bd_ref,
                  h1_ref, xn2t_ref, st_ref):
    sg = sg_ref[...]
    merged = sg[:, :D_MODEL] * _dot(hn_ref[...], wa_ref[...]) + sg[:, D_MODEL:] * _dot(att_ref[...], wb_ref[...])
    h1 = x_ref[...] + _dot(merged.astype(BF16), wo_ref[...])
    h1_ref[...] = h1
    xn2 = (h1 * lax.rsqrt(jnp.mean(h1 * h1, axis=-1, keepdims=True) + EPS)) * gf_ref[...]
    xn2t = xn2.T.astype(BF16)
    xn2t_ref[...] = xn2t
    qt = _dot(wpqt_ref[...], xn2t).astype(BF16)
    for h in range(H_P):
        st_ref[h * 2 * N_KEYS:(h + 1) * 2 * N_KEYS, :] = _dot(bd_ref[h], qt[h * D_PKEY:(h + 1) * D_PKEY, :])


def _prep_subkeys(sub_keys):
    z = jnp.zeros_like(sub_keys[:, 0])
    top = jnp.concatenate([sub_keys[:, 0], z], axis=-1)
    bot = jnp.concatenate([z, sub_keys[:, 1]], axis=-1)
    return jnp.concatenate([top, bot], axis=1).astype(BF16)


def _merge(x2d, hn, att, sg, wa, wb, wo, g_ffn, wpqt, bd, tm):
    n = x2d.shape[0]
    assert n % tm == 0
    row = lambda w: pl.BlockSpec((tm, w), lambda i: (i, 0))
    full2 = lambda a: pl.BlockSpec(a.shape, lambda i: (0, 0))
    return pl.pallas_call(
        _merge_kernel,
        grid=(n // tm,),
        in_specs=[row(D_MODEL), row(512), row(512), row(2 * D_MODEL), full2(wa), full2(wb), full2(wo),
                  pl.BlockSpec((1, D_MODEL), lambda i: (0, 0)), full2(wpqt),
                  pl.BlockSpec(bd.shape, lambda i: (0, 0, 0))],
        out_specs=[row(D_MODEL), pl.BlockSpec((D_MODEL, tm), lambda i: (0, i)),
                   pl.BlockSpec((H_P * 2 * N_KEYS, tm), lambda i: (0, i))],
        out_shape=[jax.ShapeDtypeStruct((n, D_MODEL), F32), jax.ShapeDtypeStruct((D_MODEL, n), BF16),
                   jax.ShapeDtypeStruct((H_P * 2 * N_KEYS, n), F32)],
        compiler_params=_cparams(("parallel",)),
        name="merge",
    )(x2d, hn, att, sg, wa, wb, wo, g_ffn.reshape(1, D_MODEL), wpqt, bd)


def _extract_top(s, k, first_only=True):
    rows = lax.broadcasted_iota(I32, s.shape, 0)
    cur, vals = s, []
    for _ in range(k):
        m = jnp.max(cur, axis=0, keepdims=True)
        if first_only:
            first = jnp.min(jnp.where(cur == m, rows, s.shape[0]), axis=0, keepdims=True)
            cur = jnp.where(rows == first, -jnp.inf, cur)
        else:
            cur = jnp.where(cur == m, -jnp.inf, cur)
        vals.append(m)
    return vals, cur


_STAIR = [(a, b) for a in range(P_TOPK) for b in range(P_TOPK) if (a + 1) * (b + 1) <= P_TOPK]


def _peer_prep_kernel(st_ref, s1m_ref, s2m_ref, e1_ref, e2_ref, tau_ref, top_ref, sel_ref):
    tn = st_ref.shape[1]
    taus = []

    def top_keys(rows, slot):
        def run(first_only):
            s = st_ref[rows, :]
            vals, res = _extract_top(s, P_TOPK, first_only)
            picked = jnp.where(res != s, 1.0, 0.0)
            top_ref[slot] = jnp.concatenate(vals, axis=0)
            sel_ref[slot] = picked
            return picked
        picked = run(False)
        npick = jnp.sum(picked, axis=0, keepdims=True)
        bad = jnp.sum(jnp.where(npick == float(P_TOPK), 0.0, 1.0))

        @pl.when(bad > 0.0)
        def _():
            run(True)

    for h in range(H_P):
        r1s = slice((2 * h) * N_KEYS, (2 * h + 1) * N_KEYS)
        r2s = slice((2 * h + 1) * N_KEYS, (2 * h + 2) * N_KEYS)
        top_keys(r1s, 0)
        top_keys(r2s, 1)
        s1, s2 = st_ref[r1s, :], st_ref[r2s, :]
        sel1, sel2 = sel_ref[0] > 0.0, sel_ref[1] > 0.0
        v1 = [top_ref[0, a:a + 1, :] for a in range(P_TOPK)]
        v2 = [top_ref[1, a:a + 1, :] for a in range(P_TOPK)]
        cand = [v1[a] + v2[b] for a, b in _STAIR]
        npad = (-len(cand)) % 8
        cand = jnp.concatenate(cand + [jnp.full((npad, tn), -jnp.inf, F32)], axis=0)
        top, _ = _extract_top(cand, P_TOPK)
        z = jnp.zeros_like(top[0])
        for c in top:
            z = z + jnp.exp(c - top[0])
        taus.append(top[P_TOPK - 1])
        sl = slice(h * N_KEYS, (h + 1) * N_KEYS)
        outs = ((s1m_ref, jnp.where(sel1, s1, -jnp.inf)), (s2m_ref, jnp.where(sel2, s2, -jnp.inf)),
                (e1_ref, jnp.where(sel1, jnp.exp(s1 - v1[0]), 0.0) / z),
                (e2_ref, jnp.where(sel2, jnp.exp(s2 - v2[0]), 0.0)))
        for ref, val in outs:
            for lt in range(tn // LANES):
                ref[lt, sl, :] = val[:, lt * LANES:(lt + 1) * LANES]
    tau = jnp.concatenate(taus, axis=0)
    for lt in range(tn // LANES):
        tau_ref[lt] = tau[:, lt * LANES:(lt + 1) * LANES]


def _peer_prep(st, tn):
    n = st.shape[1]
    assert n % tn == 0 and tn % LANES == 0
    hk = H_P * N_KEYS
    col = lambda r: pl.BlockSpec((tn // LANES, r, LANES), lambda i: (i, 0, 0))
    return pl.pallas_call(
        _peer_prep_kernel,
        grid=(n // tn,),
        in_specs=[pl.BlockSpec((2 * hk, tn), lambda i: (0, i))],
        out_specs=[col(hk), col(hk), col(hk), col(hk), col(H_P)],
        out_shape=[jax.ShapeDtypeStruct((n // LANES, hk, LANES), F32)] * 4
                  + [jax.ShapeDtypeStruct((n // LANES, H_P, LANES), F32)],
        scratch_shapes=[pltpu.VMEM((2, P_TOPK, tn), F32), pltpu.VMEM((2, N_KEYS, tn), F32)],
        compiler_params=_cparams(("parallel",)),
        name="peer_prep",
    )(st)


def _peer_dense_kernel(u_ref, vt_ref, xn2t_ref, s1m_ref, s2m_ref, e1_ref, e2_ref, tau_ref, h1_ref, gf_ref,
                       y_ref, acc_ref, a_ref, yb_ref):
    e = pl.program_id(1)
    te, tn = u_ref.shape[0], xn2t_ref.shape[1]
    nblk, nlt = te // N_KEYS, tn // LANES

    @pl.when(e == 0)
    def _():
        acc_ref[...] = jnp.zeros(acc_ref.shape, F32)

    a = _dot(u_ref[...], xn2t_ref[...])
    for lt in range(nlt):
        a_ref[lt] = a[:, lt * LANES:(lt + 1) * LANES]

    def token_tile(lt, carry):
        def expert_tile(blk, carry2):
            b0 = pl.multiple_of(blk * N_KEYS, N_KEYS)
            a_idx = e * nblk + blk
            w = jnp.zeros((N_KEYS, LANES), F32)
            for h in range(H_P):
                sl = slice(h * N_KEYS, (h + 1) * N_KEYS)
                s1row = s1m_ref[lt, pl.ds(h * N_KEYS + a_idx, 1), :]
                e1row = e1_ref[lt, pl.ds(h * N_KEYS + a_idx, 1), :]
                t = s2m_ref[lt, sl, :] + s1row
                w = w + jnp.where(t >= tau_ref[lt, h:h + 1, :], e2_ref[lt, sl, :], 0.0) * e1row
            a_t = a_ref[lt, pl.ds(b0, N_KEYS), :]
            gl = 0.5 * a_t * (1.0 + lax.erf(a_t * (2.0 ** -0.5)))
            yb_ref[lt, pl.ds(b0, N_KEYS), :] = (w * gl).astype(BF16)
            return carry2

        lax.fori_loop(0, nblk, expert_tile, 0)
        return carry

    lax.fori_loop(0, nlt, token_tile, 0)
    yb = jnp.concatenate([yb_ref[lt] for lt in range(nlt)], axis=1)
    acc_ref[...] += _dot(vt_ref[...], yb)

    @pl.when(e == pl.num_programs(1) - 1)
    def _():
        h2 = h1_ref[...] + acc_ref[...].T
        y_ref[...] = (h2 * lax.rsqrt(jnp.mean(h2 * h2, axis=-1, keepdims=True) + EPS)) * gf_ref[...]


def _peer_dense(ub, vtb, xn2t, s1m, s2m, e1, e2, tau, h1, g_final, tn, te):
    n = xn2t.shape[1]
    ne = ub.shape[0]
    assert n % tn == 0 and ne % te == 0 and tn % LANES == 0 and te % N_KEYS == 0
    hk = H_P * N_KEYS
    col = lambda r: pl.BlockSpec((tn // LANES, r, LANES), lambda i, e: (i, 0, 0))
    row = lambda w: pl.BlockSpec((tn, w), lambda i, e: (i, 0))
    return pl.pallas_call(
        _peer_dense_kernel,
        grid=(n // tn, ne // te),
        in_specs=[pl.BlockSpec((te, D_MODEL), lambda i, e: (e, 0)),
                  pl.BlockSpec((D_MODEL, te), lambda i, e: (0, e)),
                  pl.BlockSpec((D_MODEL, tn), lambda i, e: (0, i)),
                  col(hk), col(hk), col(hk), col(hk), col(H_P), row(D_MODEL),
                  pl.BlockSpec((1, D_MODEL), lambda i, e: (0, 0))],
        out_specs=row(D_MODEL),
        out_shape=jax.ShapeDtypeStruct((n, D_MODEL), F32),
        scratch_shapes=[pltpu.VMEM((D_MODEL, tn), F32), pltpu.VMEM((tn // LANES, te, LANES), F32),
                        pltpu.VMEM((tn // LANES, te, LANES), BF16)],
        compiler_params=_cparams(("parallel", "arbitrary")),
        name="peer_dense",
    )(ub, vtb, xn2t, s1m, s2m, e1, e2, tau, h1, g_final.reshape(1, D_MODEL))


def _row_tile(n, pref):
    t = pref
    while n % t:
        t //= 2
    return t


def _group(x, pos, tables_rows, attend, c0, n0, m0, lw, fw, g_final):
    nb, t, _ = x.shape
    n = nb * t
    g_mix, wp, bias_row, g_mnorm, wa, wb, wo = lw
    g_ffn, wpqt, bd, ub, vtb = fw
    x2d = x.reshape(n, D_MODEL)
    tm = _row_tile(n, 256)
    tables = _rope_tables(pos)
    if tables_rows != t:
        tables = tuple(jnp.tile(a, (tables_rows // t, 1)) for a in tables)
    (mq, mk, mv, mo, aq, ak, av, akb, avb, iq, sg, sm, ik, ik2) = _inproj(x2d, g_mix, wp, bias_row, tables, tm)
    hn, cc, nn, mm = _mlstm(mq, mk, mv, sm, mo, g_mnorm, c0, n0, m0, nb, t)
    att = attend(aq, iq, sm, ak, av, akb, avb, ik, ik2)
    h1, xn2t, st = _merge(x2d, hn, att, sg, wa, wb, wo, g_ffn, wpqt, bd, tm)
    tn = _row_tile(n, 512)
    s1m, s2m, e1, e2, tau = _peer_prep(st, _row_tile(n, 256))
    y = _peer_dense(ub, vtb, xn2t, s1m, s2m, e1, e2, tau, h1, g_final, tn, 512)
    return (y.reshape(nb, t, D_MODEL), ak.reshape(nb, t, H_A, DH_A), av.reshape(nb, t, H_A, DH_A),
            ik.reshape(nb, t, D_IDX), cc, nn, mm)


def kernel(x_prompt, x_sample, cache_k, cache_v, cache_kidx, page_table, state_C, state_n, state_m,
           g_mix, w_in, b_mgate, g_mnorm, w_a, w_b, w_o, g_ffn, w_pq, sub_keys, peer_u, peer_v, g_final):
    depth = w_in.shape[0]
    assert depth == 1, "single-layer step"
    l = 0
    bp, sp, _ = x_prompt.shape
    bd_, td, _ = x_sample.shape
    past = page_table.shape[1] * PAGE_SIZE

    one = lambda a: a.reshape(a.shape[1:])
    bias_row = jnp.zeros((1, LANES), F32).at[0, SM_IP:SM_IW].set(b_mgate[l].astype(F32))
    lw = (g_mix[l], _prep_w_in(one(w_in)), bias_row, g_mnorm[l],
          one(w_a).astype(BF16), one(w_b).astype(BF16), one(w_o).astype(BF16))
    fw = (g_ffn[l], one(w_pq).T.astype(BF16), _prep_subkeys(one(sub_keys)),
          one(peer_u).astype(BF16), one(peer_v).T.astype(BF16))

    def attend_p(aq, iq, sm, ak, av, akb, avb, ik, ik2):
        return _dsa_prompt(aq, iq, sm, akb, avb, ik2, bp, sp)

    def attend_s(aq, iq, sm, ak, av, akb, avb, ik, ik2):
        return _dsa_sample(aq, iq, sm, ak, av, ik, one(cache_k), one(cache_v), one(cache_kidx), page_table, bd_, td)

    zc = jnp.zeros((bp, H_M, DV_M, DK_M), F32)
    zn = jnp.zeros((bp, H_M, DK_M), F32)
    zm = jnp.zeros((bp, H_M), F32)
    yp, kp, vp, kip, cp, np_, mp = _group(x_prompt, jnp.arange(sp), sp, attend_p, zc, zn, zm, lw, fw, g_final)
    ys, ks, vs, kis, cs, ns, ms = _group(x_sample, past + jnp.arange(td), bd_ * td, attend_s,
                                         one(state_C), one(state_n), one(state_m), lw, fw, g_final)
    st = lambda a: a[None]
    return (yp, ys, st(kp), st(vp), st(kip), st(cp), st(np_), st(mp),
            st(ks), st(vs), st(kis), st(cs), st(ns), st(ms))
```

```python
import functools
import math

import jax
import jax.numpy as jnp
from jax import lax
from jax.experimental import pallas as pl
from jax.experimental.pallas import tpu as pltpu

F32 = jnp.float32
BF16 = jnp.bfloat16
I32 = jnp.int32

D_MODEL = 1024
PAGE_SIZE = 128
H_M, DK_M, DV_M, M_CHUNK = 4, 128, 128, 64
H_A, DH_A = 4, 128
H_I, D_IDX = 8, 64
TOPK_MAX = 256
Q_BLOCK = 128
ROPE_THETA = 10000.0
N_KEYS = 128
H_P, D_PKEY, P_TOPK = 8, 128, 16
EPS = 1e-6

MIX_SPLITS = (H_M * DK_M, H_M * DK_M, H_M * DV_M, H_M, H_M, H_M * DV_M,
              H_A * DH_A, H_A * DH_A, H_A * DH_A, H_I * D_IDX, D_IDX, H_I, 2 * D_MODEL)

LANES = 128
VMEM_LIMIT_BYTES = 56 * 1024 * 1024
NEG_INF_KEY = -2139095041
INT_MIN = -2147483648
NEG_BIG = -1e30

SM_IK, SM_IP, SM_LF, SM_IW, SM_END = 0, 64, 68, 72, 80
W_MQ, W_MK, W_MV, W_MO, W_AQ, W_AK, W_AV, W_IQ, W_G, W_SM, W_COLS = (
    0, 512, 1024, 1536, 2048, 2560, 3072, 3584, 4096, 6144, 6272)


def _dot(a, b):
    return jnp.dot(a, b, preferred_element_type=F32)


def _dot_nt(a, b):
    return lax.dot_general(a, b, (((1,), (1,)), ((), ())), preferred_element_type=F32)


def _dot_tn(a, b):
    return lax.dot_general(a, b, (((0,), (0,)), ((), ())), preferred_element_type=F32)


def _okey(x):
    i = lax.bitcast_convert_type(x, I32)
    return i ^ ((i >> 31) & jnp.int32(0x7FFFFFFF))


def _cparams(sem):
    return pltpu.CompilerParams(dimension_semantics=sem, vmem_limit_bytes=VMEM_LIMIT_BYTES)


def _inproj_kernel(x_ref, g_ref, w_ref, bias_ref, cosa_ref, sina_ref, cosi_ref, sini_ref,
                   mq_ref, mk_ref, mv_ref, mo_ref, aq_ref, ak_ref, av_ref, akb_ref, avb_ref,
                   iq_ref, sg_ref, sm_ref, ik_ref, ik2_ref):
    x = x_ref[...]
    ms = jnp.mean(x * x, axis=-1, keepdims=True)
    xn = ((x * lax.rsqrt(ms + EPS)) * g_ref[...]).astype(BF16)

    def proj(lo, width):
        return _dot(xn, w_ref[:, lo:lo + width])

    cosa, sina = cosa_ref[...], sina_ref[...]
    cosi, sini = cosi_ref[...], sini_ref[...]
    lane = lax.broadcasted_iota(I32, (x.shape[0], LANES), 1)
    first_half = (lane % 64) < 32

    def rope128(z):
        outs = []
        for h in range(z.shape[1] // LANES):
            zh = z[:, h * LANES:(h + 1) * LANES]
            outs.append(zh * cosa + pltpu.roll(zh, 64, 1) * sina)
        return jnp.concatenate(outs, axis=1) if len(outs) > 1 else outs[0]

    def rope64_slab(zh):
        rot = jnp.where(first_half, pltpu.roll(zh, 96, 1), pltpu.roll(zh, 32, 1))
        return zh * cosi + rot * sini

    mq_ref[...] = (proj(W_MQ, 512) * (DK_M ** -0.5)).astype(BF16)
    mk_ref[...] = proj(W_MK, 512).astype(BF16)
    mv_ref[...] = proj(W_MV, 512).astype(BF16)
    mo_ref[...] = proj(W_MO, 512)
    aq_ref[...] = rope128(proj(W_AQ, 512)).astype(BF16)
    ak = rope128(proj(W_AK, 512))
    ak_ref[...] = ak
    akb_ref[...] = ak.astype(BF16)
    av = proj(W_AV, 512)
    av_ref[...] = av
    avb_ref[...] = av.astype(BF16)
    ziq = proj(W_IQ, 512)
    iq = jnp.concatenate([rope64_slab(ziq[:, h * LANES:(h + 1) * LANES]) for h in range(4)], axis=1)
    iq_ref[...] = (iq * (D_IDX ** -0.5)).astype(BF16)
    sg_ref[...] = jax.nn.sigmoid(proj(W_G, 2 * D_MODEL))
    zs = proj(W_SM, LANES)
    ikr = rope64_slab(zs)
    zb = zs + bias_ref[...]
    logsig = jnp.minimum(zb, 0.0) - jnp.log1p(jnp.exp(-jnp.abs(zb)))
    sm = jnp.where(lane < SM_IP, ikr,
                   jnp.where(lane < SM_LF, zb,
                             jnp.where(lane < SM_IW, logsig,
                                       jnp.where(lane < SM_END, zs * (H_I ** -0.5), 0.0))))
    sm_ref[...] = sm
    ik_ref[...] = ikr[:, :D_IDX]
    ik2_ref[...] = jnp.where(lane < D_IDX, ikr, pltpu.roll(ikr, 64, 1)).astype(BF16)


def _prep_w_in(w_in):
    parts, o = [], 0
    for s in MIX_SPLITS:
        parts.append(w_in[:, o:o + s])
        o += s
    mq, mk, mv, mi, mf, mo, aq, ak, av, iq, ik, iw, gates = parts
    pad = jnp.zeros((w_in.shape[0], LANES - SM_END), w_in.dtype)
    small = jnp.concatenate([ik, mi, mf, iw, pad], axis=1)
    return jnp.concatenate([mq, mk, mv, mo, aq, ak, av, iq, gates, small], axis=1).astype(BF16)


def _rope_tables(pos):
    pos = pos.astype(F32)[:, None]

    def tab(d):
        inv = ROPE_THETA ** (-jnp.arange(0, d, 2, dtype=F32) / d)
        ang = pos * inv[None, :]
        c, s = jnp.cos(ang), jnp.sin(ang)
        cos = jnp.concatenate([c, c], axis=1)
        sin = jnp.concatenate([-s, s], axis=1)
        reps = LANES // d
        return jnp.tile(cos, (1, reps)), jnp.tile(sin, (1, reps))

    cosa, sina = tab(DH_A)
    cosi, sini = tab(D_IDX)
    return cosa, sina, cosi, sini


def _inproj(x2d, g_mix, wp, bias_row, tables, tm):
    n = x2d.shape[0]
    t_tab = tables[0].shape[0]
    assert n % tm == 0 and t_tab % tm == 0
    nt = t_tab // tm
    row = lambda i: (i, 0)
    tabspec = pl.BlockSpec((tm, LANES), lambda i: (i % nt, 0))
    full = lambda shape: pl.BlockSpec(shape, lambda i: (0, 0))
    out_defs = [
        (512, BF16), (512, BF16), (512, BF16), (512, F32), (512, BF16), (512, F32), (512, F32),
        (512, BF16), (512, BF16), (512, BF16), (2 * D_MODEL, F32), (LANES, F32), (D_IDX, F32), (LANES, BF16)]
    return pl.pallas_call(
        _inproj_kernel,
        grid=(n // tm,),
        in_specs=[pl.BlockSpec((tm, D_MODEL), row), full((1, D_MODEL)), full((D_MODEL, W_COLS)),
                  full((1, LANES)), tabspec, tabspec, tabspec, tabspec],
        out_specs=[pl.BlockSpec((tm, w), row) for w, _ in out_defs],
        out_shape=[jax.ShapeDtypeStruct((n, w), dt) for w, dt in out_defs],
        compiler_params=_cparams(("parallel",)),
        name="inproj",
    )(x2d, g_mix.reshape(1, D_MODEL), wp, bias_row, *tables)


def _mlstm_kernel(q_ref, k_ref, v_ref, sm_ref, mo_ref, gm_ref, c0_ref, n0_ref, m0_ref,
                  hn_ref, c_ref, n_ref, m_ref, *, c, nchunks):
    @pl.when(pl.program_id(1) == 0)
    def _():
        c_ref[...] = c0_ref[...]
        n_ref[...] = n0_ref[...]
        m_ref[...] = m0_ref[...]

    ri = lax.broadcasted_iota(I32, (c, c), 0)
    ci = lax.broadcasted_iota(I32, (c, c), 1)
    eye = ri == ci
    tri = ri >= ci

    def to_row(col):
        return jnp.sum(jnp.where(eye, jnp.broadcast_to(col, (c, c)), 0.0), axis=0, keepdims=True)

    def chunk(rows):
        q = q_ref[0, rows, :]
        k = k_ref[0, rows, :]
        v = v_ref[0, rows, :]
        sm = sm_ref[0, rows, :]
        mo = mo_ref[0, rows, :]
        for h in range(H_M):
            sl = slice(h * LANES, (h + 1) * LANES)
            qh, kh, vh = q[:, sl], k[:, sl], v[:, sl]
            ipc = sm[:, SM_IP + h:SM_IP + h + 1]
            lfc = sm[:, SM_LF + h:SM_LF + h + 1]
            lf_row = to_row(lfc)
            ip_row = to_row(ipc)
            b_col = jnp.sum(jnp.where(tri, jnp.broadcast_to(lf_row, (c, c)), 0.0), axis=1, keepdims=True)
            b_row = to_row(b_col)
            m_prev = m_ref[0, h][:, :1]
            n_prev = n_ref[0, h]
            c_prev = c_ref[0, h]
            m_inter = b_col + m_prev
            dmat = jnp.where(tri, b_col - b_row + ip_row, -jnp.inf)
            m_t = jnp.maximum(m_inter, jnp.max(dmat, axis=1, keepdims=True))
            w_inter = jnp.exp(m_inter - m_t)
            s = _dot_nt(qh, kh) * jnp.exp(dmat - m_t)
            num = w_inter * _dot_nt(qh, c_prev.astype(BF16)) + _dot(s.astype(BF16), vh)
            den = (w_inter * jnp.sum(qh.astype(F32) * n_prev, axis=1, keepdims=True)
                   + jnp.sum(s, axis=1, keepdims=True))
            hh = num / jnp.maximum(jnp.abs(den), jnp.exp(-m_t))
            m_new = m_t[c - 1:c, :]
            b_last = b_col[c - 1:c, :]
            g_state = jnp.exp(b_last + m_prev - m_new)
            g_tok = jnp.exp(b_last - b_col + ipc - m_new)
            gv = (g_tok * vh.astype(F32)).astype(BF16)
            c_ref[0, h] = g_state * c_prev + _dot_tn(gv, kh)
            n_ref[0, h] = g_state * n_prev + jnp.sum(g_tok * kh.astype(F32), axis=0, keepdims=True)
            m_ref[0, h] = jnp.broadcast_to(m_new, (1, LANES))
            y = hh * lax.rsqrt(jnp.mean(hh * hh, axis=1, keepdims=True) + EPS) * gm_ref[:, sl]
            hn_ref[0, rows, sl] = (jax.nn.sigmoid(mo[:, sl]) * y).astype(BF16)

    if nchunks == 1:
        chunk(slice(0, c))
    else:
        def body(i, carry):
            chunk(pl.ds(pl.multiple_of(i * c, c), c))
            return carry
        lax.fori_loop(0, nchunks, body, 0)


def _mlstm(mq, mk, mv, sm, mo, g_mnorm, c0, n0, m0, nb, t):
    c = math.gcd(M_CHUNK, t)
    tb = min(t, 8 * c)
    assert t % tb == 0
    r3 = lambda a: a.reshape(nb, t, a.shape[-1])
    n0b = n0.reshape(nb, H_M, 1, DK_M).astype(F32)
    m0b = jnp.broadcast_to(m0.astype(F32)[:, :, None, None], (nb, H_M, 1, LANES))
    tok = lambda w: pl.BlockSpec((1, tb, w), lambda b, j: (b, j, 0))
    st4 = lambda shape: pl.BlockSpec(shape, lambda b, j: (b, 0, 0, 0))
    hn, cc, nn, mm = pl.pallas_call(
        functools.partial(_mlstm_kernel, c=c, nchunks=tb // c),
        grid=(nb, t // tb),
        in_specs=[tok(512), tok(512), tok(512), tok(LANES), tok(512),
                  pl.BlockSpec((1, H_M * DV_M), lambda b, j: (0, 0)),
                  st4((1, H_M, DV_M, DK_M)), st4((1, H_M, 1, DK_M)), st4((1, H_M, 1, LANES))],
        out_specs=[tok(512), st4((1, H_M, DV_M, DK_M)), st4((1, H_M, 1, DK_M)), st4((1, H_M, 1, LANES))],
        out_shape=[jax.ShapeDtypeStruct((nb, t, H_M * DV_M), BF16),
                   jax.ShapeDtypeStruct((nb, H_M, DV_M, DK_M), F32),
                   jax.ShapeDtypeStruct((nb, H_M, 1, DK_M), F32),
                   jax.ShapeDtypeStruct((nb, H_M, 1, LANES), F32)],
        compiler_params=_cparams(("parallel", "arbitrary")),
        name="mlstm",
    )(r3(mq), r3(mk), r3(mv), r3(sm), r3(mo), g_mnorm.reshape(1, H_M * DV_M),
      c0.astype(F32), n0b, m0b)
    return hn.reshape(nb * t, H_M * DV_M), cc, nn.reshape(nb, H_M, DK_M), mm[:, :, 0, 0]


def _kth_largest_key(count_ge, k, shape):
    kf = jnp.float32(k)
    r0 = jnp.where(count_ge(jnp.zeros(shape, I32)) >= kf, jnp.int32(0), jnp.int32(INT_MIN))

    def body(t, r):
        cand = r + lax.shift_left(jnp.int32(1), jnp.int32(30) - t)
        return jnp.where(count_ge(cand) >= kf, cand, r)

    return lax.fori_loop(0, 31, body, r0)


def _dsa_prompt_kernel(aq_ref, iq_ref, sm_ref, kb_ref, vb_ref, ik2_ref, out_ref, skey_ref, acc_ref, *, kc, topk):
    i = pl.program_id(1)
    qb = Q_BLOCK
    nk = (i * qb + qb + kc - 1) // kc
    iq = iq_ref[0]
    lane = lax.broadcasted_iota(I32, (qb, LANES), 1)
    zero = jnp.zeros((qb, LANES), BF16)
    lhs = []
    for hp in range(H_I // 2):
        slab = iq[:, hp * LANES:(hp + 1) * LANES]
        lhs.append(jnp.where(lane < D_IDX, slab, zero))
        lhs.append(jnp.where(lane >= D_IDX, slab, zero))
    sm = sm_ref[0]
    qpos = i * qb + lax.broadcasted_iota(I32, (qb, 1), 0)

    def score_body(cidx, carry):
        k0 = pl.multiple_of(cidx * kc, kc)
        kk = ik2_ref[0, pl.ds(k0, kc), :]
        sc = jnp.zeros((qb, kc), F32)
        for h in range(H_I):
            sc = sc + sm[:, SM_IW + h:SM_IW + h + 1] * jnp.maximum(_dot_nt(lhs[h], kk), 0.0)
        kpos = k0 + lax.broadcasted_iota(I32, (1, kc), 1)
        sc = jnp.where(kpos <= qpos, sc, -jnp.inf)
        skey_ref[:, pl.ds(k0, kc)] = _okey(sc)
        return carry

    lax.fori_loop(0, nk, score_body, 0)

    def count_ge(cand):
        def body(cidx, acc):
            k0 = pl.multiple_of(cidx * kc, kc)
            ks = skey_ref[:, pl.ds(k0, kc)]
            m = jnp.where(ks >= cand, 1.0, 0.0)
            part = m[:, :LANES]
            for t in range(1, kc // LANES):
                part = part + m[:, t * LANES:(t + 1) * LANES]
            return acc + part
        acc = lax.fori_loop(0, nk, body, jnp.zeros((qb, LANES), F32))
        return jnp.sum(acc, axis=1, keepdims=True)

    thr = _kth_largest_key(count_ge, topk, (qb, 1))
    thr = jnp.maximum(thr, jnp.int32(NEG_INF_KEY + 1))
    scale = DH_A ** -0.5
    aq = aq_ref[0]
    acc_ref[...] = jnp.zeros(acc_ref.shape, F32)

    def att_body(cidx, carry):
        k0 = pl.multiple_of(cidx * kc, kc)
        sel = skey_ref[:, pl.ds(k0, kc)] >= thr
        out = []
        for h in range(H_A):
            sl = slice(h * LANES, (h + 1) * LANES)
            m_i, l_i = carry[2 * h], carry[2 * h + 1]
            lg = jnp.where(sel, _dot_nt(aq[:, sl], kb_ref[0, pl.ds(k0, kc), sl]) * scale, NEG_BIG)
            m_new = jnp.maximum(m_i, jnp.max(lg, axis=1, keepdims=True))
            alpha = jnp.exp(m_i - m_new)
            p = jnp.where(sel, jnp.exp(lg - m_new), 0.0)
            out += [m_new, alpha * l_i + jnp.sum(p, axis=1, keepdims=True)]
            acc_ref[h] = alpha * acc_ref[h] + _dot(p.astype(BF16), vb_ref[0, pl.ds(k0, kc), sl])
        return tuple(out)

    init = (jnp.full((qb, 1), NEG_BIG, F32), jnp.zeros((qb, 1), F32)) * H_A
    fin = lax.fori_loop(0, nk, att_body, init)
    for h in range(H_A):
        out_ref[0, :, h * LANES:(h + 1) * LANES] = (acc_ref[h] / fin[2 * h + 1]).astype(BF16)


def _dsa_prompt(aq, iq, sm, akb, avb, ik2, nb, s):
    topk = min(TOPK_MAX, s // 4)
    kc = min(512, s)
    assert s % kc == 0 and s % Q_BLOCK == 0 and kc % Q_BLOCK == 0
    r3 = lambda a: a.reshape(nb, s, a.shape[-1])
    qspec = lambda w: pl.BlockSpec((1, Q_BLOCK, w), lambda b, i: (b, i, 0))
    kspec = lambda w: pl.BlockSpec((1, s, w), lambda b, i: (b, 0, 0))
    out = pl.pallas_call(
        functools.partial(_dsa_prompt_kernel, kc=kc, topk=topk),
        grid=(nb, s // Q_BLOCK),
        in_specs=[qspec(512), qspec(512), qspec(LANES), kspec(512), kspec(512), kspec(LANES)],
        out_specs=qspec(512),
        out_shape=jax.ShapeDtypeStruct((nb, s, H_A * DH_A), BF16),
        scratch_shapes=[pltpu.VMEM((Q_BLOCK, s), I32), pltpu.VMEM((H_A, Q_BLOCK, DH_A), F32)],
        compiler_params=_cparams(("parallel", "arbitrary")),
        name="dsa_prompt",
    )(r3(aq), r3(iq), r3(sm), r3(akb), r3(avb), r3(ik2))
    return out.reshape(nb * s, H_A * DH_A)


def _idx_lhs(iq_ref):
    iqf = iq_ref[0].astype(F32)
    return jnp.concatenate([iqf[:, h * D_IDX:(h + 1) * D_IDX] for h in range(H_I)], axis=0).astype(BF16)


def _idx_score(r, sm, t):
    sc = jnp.zeros((t, r.shape[1]), F32)
    for h in range(H_I):
        sc = sc + sm[:, SM_IW + h:SM_IW + h + 1] * jnp.maximum(r[h * t:(h + 1) * t, :], 0.0)
    return sc


def _dsa_s_score_kernel(pt_ref, iq_ref, sm_ref, ikn_ref, *rest, pgs, t, topk, npast):
    pages = rest[:pgs]
    skey_ref, skn_ref, thr_ref = rest[pgs:]
    j = pl.program_id(1)
    lhs = _idx_lhs(iq_ref)
    sm = sm_ref[0]
    kall = jnp.concatenate([pages[r][0] for r in range(pgs)], axis=0).astype(BF16)
    base = pl.multiple_of(j * (pgs * PAGE_SIZE), pgs * PAGE_SIZE)
    skey_ref[0, :, pl.ds(base, pgs * PAGE_SIZE)] = _okey(_idx_score(_dot_nt(lhs, kall), sm, t))

    @pl.when(j == pl.num_programs(1) - 1)
    def _():
        scn = _idx_score(_dot_nt(lhs, ikn_ref[0]), sm, t)
        ti = lax.broadcasted_iota(I32, (t, LANES), 0)
        si = lax.broadcasted_iota(I32, (t, LANES), 1)
        kn = _okey(jnp.where(si <= ti, scn, -jnp.inf))
        skn_ref[0] = kn
        cw = 1024 if npast % 1024 == 0 else PAGE_SIZE

        def count_ge(cand):
            def body(cidx, acc):
                k0 = pl.multiple_of(cidx * cw, cw)
                m = jnp.where(skey_ref[0, :, pl.ds(k0, cw)] >= cand, 1.0, 0.0)
                part = m[:, :LANES]
                for u in range(1, cw // LANES):
                    part = part + m[:, u * LANES:(u + 1) * LANES]
                return acc + part
            acc = lax.fori_loop(0, npast // cw, body, jnp.where(kn >= cand, 1.0, 0.0))
            return jnp.sum(acc, axis=1, keepdims=True)

        thr = _kth_largest_key(count_ge, topk, (t, 1))
        thr = jnp.maximum(thr, jnp.int32(NEG_INF_KEY + 1))
        thr_ref[0] = jnp.broadcast_to(thr, (t, LANES))


def _dsa_s_att_kernel(pt_ref, aq_ref, skey_ref, skn_ref, thr_ref, kn_ref, vn_ref, emat_ref, *rest, pg, t):
    kpages, vpages = rest[:pg], rest[pg:2 * pg]
    out_ref, m_sc, l_sc, acc_sc = rest[2 * pg:]
    j = pl.program_id(1)

    @pl.when(j == 0)
    def _():
        m_sc[...] = jnp.full(m_sc.shape, NEG_BIG, F32)
        l_sc[...] = jnp.zeros(l_sc.shape, F32)
        acc_sc[...] = jnp.zeros(acc_sc.shape, F32)

    scale = DH_A ** -0.5
    aqf = aq_ref[0].astype(F32)
    qs = jnp.concatenate([aqf[:, h * LANES:(h + 1) * LANES] for h in range(H_A)], axis=0).astype(BF16)
    thr = thr_ref[0][:, :1]
    rows = lambda a, h: a[h * t:(h + 1) * t]

    def update(lg, sel, pv_of):
        lg = jnp.where(sel, lg * scale, NEG_BIG)
        m_i, l_i = m_sc[:, :1], l_sc[:, :1]
        m_new = jnp.maximum(m_i, jnp.max(lg, axis=1, keepdims=True))
        alpha = jnp.exp(m_i - m_new)
        p = jnp.where(sel, jnp.exp(lg - m_new), 0.0)
        l_new = alpha * l_i + jnp.sum(p, axis=1, keepdims=True)
        acc_sc[...] = alpha * acc_sc[...] + pv_of(p.astype(BF16))
        m_sc[...] = jnp.broadcast_to(m_new, m_sc.shape)
        l_sc[...] = jnp.broadcast_to(l_new, l_sc.shape)

    ncol = pg * PAGE_SIZE * H_A
    kall = jnp.concatenate([kpages[r][0] for r in range(pg)], axis=0).astype(BF16)
    vall = jnp.concatenate([vpages[r][0] for r in range(pg)], axis=0).astype(BF16)
    picked = jnp.where(skey_ref[0] >= thr, 1.0, 0.0)
    picked = jnp.concatenate([picked] * H_A, axis=0).astype(BF16)
    spread = jnp.concatenate([_dot(picked[:, r * PAGE_SIZE:(r + 1) * PAGE_SIZE], emat_ref[...])
                              for r in range(pg)], axis=1)
    col_head = lax.broadcasted_iota(I32, (H_A * t, ncol), 1) & (H_A - 1)
    row_head = lax.shift_right_logical(lax.broadcasted_iota(I32, (H_A * t, ncol), 0), int(math.log2(t)))
    sel_past = jnp.where(col_head == row_head, spread, 0.0) > 0.5
    update(_dot_nt(qs, kall), sel_past, lambda pb: _dot(pb, vall))

    @pl.when(j == pl.num_programs(1) - 1)
    def _():
        sel1 = jnp.where(skn_ref[0] >= thr, 1, 0)
        sel_new = jnp.concatenate([sel1] * H_A, axis=0) > 0
        hs = lambda ref, h: ref[0, :, h * LANES:(h + 1) * LANES]
        lg = jnp.concatenate([rows(_dot_nt(qs, hs(kn_ref, h)), h) for h in range(H_A)], axis=0)
        update(lg, sel_new,
               lambda pb: jnp.concatenate([rows(_dot(pb, hs(vn_ref, h)), h) for h in range(H_A)], axis=0))
        o = acc_sc[...] / l_sc[:, :1]
        for h in range(H_A):
            out_ref[0, :, h * LANES:(h + 1) * LANES] = rows(o, h).astype(BF16)


def _dsa_sample(aq, iq, sm, ak, av, ik, cache_k, cache_v, cache_kidx, page_table, nb, t):
    n_pages = page_table.shape[1]
    npast = n_pages * PAGE_SIZE
    topk = min(TOPK_MAX, (npast + t) // 4)
    pg = math.gcd(8, n_pages)
    pgs = math.gcd(32, n_pages)
    assert t & (t - 1) == 0 and H_A & (H_A - 1) == 0
    n_pool = cache_k.shape[0]
    ck = cache_k.reshape(n_pool, PAGE_SIZE * H_A, DH_A)
    cv = cache_v.reshape(n_pool, PAGE_SIZE * H_A, DH_A)
    emat = (jnp.arange(PAGE_SIZE * H_A)[None, :] // H_A == jnp.arange(PAGE_SIZE)[:, None]).astype(BF16)
    r3 = lambda a: a.reshape(nb, t, a.shape[-1])
    padrows = lambda a: jnp.pad(r3(a), ((0, 0), (0, LANES - t), (0, 0))).astype(BF16)
    tokspec = lambda w: pl.BlockSpec((1, t, w), lambda b, j, pt: (b, 0, 0))
    newspec = lambda w: pl.BlockSpec((1, LANES, w), lambda b, j, pt: (b, 0, 0))

    def pagespec(rows_, w, r, per_step):
        return pl.BlockSpec((1, rows_, w), lambda b, j, pt, r=r: (pt[b, j * per_step + r], 0, 0))

    skey, skn, thr = pl.pallas_call(
        functools.partial(_dsa_s_score_kernel, pgs=pgs, t=t, topk=topk, npast=npast),
        grid_spec=pltpu.PrefetchScalarGridSpec(
            num_scalar_prefetch=1,
            grid=(nb, n_pages // pgs),
            in_specs=[tokspec(512), tokspec(LANES), newspec(D_IDX)]
                     + [pagespec(PAGE_SIZE, D_IDX, r, pgs) for r in range(pgs)],
            out_specs=[tokspec(npast), tokspec(LANES), tokspec(LANES)],
        ),
        out_shape=[jax.ShapeDtypeStruct((nb, t, npast), I32),
                   jax.ShapeDtypeStruct((nb, t, LANES), I32),
                   jax.ShapeDtypeStruct((nb, t, LANES), I32)],
        compiler_params=_cparams(("parallel", "arbitrary")),
        name="dsa_sample_score",
    )(page_table, r3(iq), r3(sm), padrows(ik), *([cache_kidx] * pgs))

    att = pl.pallas_call(
        functools.partial(_dsa_s_att_kernel, pg=pg, t=t),
        grid_spec=pltpu.PrefetchScalarGridSpec(
            num_scalar_prefetch=1,
            grid=(nb, n_pages // pg),
            in_specs=[tokspec(512),
                      pl.BlockSpec((1, t, pg * PAGE_SIZE), lambda b, j, pt: (b, 0, j)),
                      tokspec(LANES), tokspec(LANES), newspec(512), newspec(512),
                      pl.BlockSpec(emat.shape, lambda b, j, pt: (0, 0))]
                     + [pagespec(PAGE_SIZE * H_A, DH_A, r, pg) for r in range(pg)] * 2,
            out_specs=tokspec(512),
            scratch_shapes=[pltpu.VMEM((H_A * t, LANES), F32)] * 3,
        ),
        out_shape=jax.ShapeDtypeStruct((nb, t, H_A * DH_A), BF16),
        compiler_params=_cparams(("parallel", "arbitrary")),
        name="dsa_sample_att",
    )(page_table, r3(aq), skey, skn, thr, padrows(ak), padrows(av), emat, *([ck] * pg), *([cv] * pg))
    return att.reshape(nb * t, H_A * DH_A)


def _merge_kernel(x_ref, hn_ref, att_ref, sg_ref, wa_ref, wb_ref, wo_ref, gf_ref, wpqt_ref, bd_ref,
                  h1_ref, xn2t_ref, st_ref):
    sg = sg_ref[...]
    merged = sg[:, :D_MODEL] * _dot(hn_ref[...], wa_ref[...]) + sg[:, D_MODEL:] * _dot(att_ref[...], wb_ref[...])
    h1 = x_ref[...] + _dot(merged.astype(BF16), wo_ref[...])
    h1_ref[...] = h1
    xn2 = (h1 * lax.rsqrt(jnp.mean(h1 * h1, axis=-1, keepdims=True) + EPS)) * gf_ref[...]
    xn2t = xn2.T.astype(BF16)
    xn2t_ref[...] = xn2t
    qt = _dot(wpqt_ref[...], xn2t).astype(BF16)
    for h in range(H_P):
        st_ref[h * 2 * N_KEYS:(h + 1) * 2 * N_KEYS, :] = _dot(bd_ref[h], qt[h * D_PKEY:(h + 1) * D_PKEY, :])


def _prep_subkeys(sub_keys):
    z = jnp.zeros_like(sub_keys[:, 0])
    top = jnp.concatenate([sub_keys[:, 0], z], axis=-1)
    bot = jnp.concatenate([z, sub_keys[:, 1]], axis=-1)
    return jnp.concatenate([top, bot], axis=1).astype(BF16)


def _merge(x2d, hn, att, sg, wa, wb, wo, g_ffn, wpqt, bd, tm):
    n = x2d.shape[0]
    assert n % tm == 0
    row = lambda w: pl.BlockSpec((tm, w), lambda i: (i, 0))
    full2 = lambda a: pl.BlockSpec(a.shape, lambda i: (0, 0))
    return pl.pallas_call(
        _merge_kernel,
        grid=(n // tm,),
        in_specs=[row(D_MODEL), row(512), row(512), row(2 * D_MODEL), full2(wa), full2(wb), full2(wo),
                  pl.BlockSpec((1, D_MODEL), lambda i: (0, 0)), full2(wpqt),
                  pl.BlockSpec(bd.shape, lambda i: (0, 0, 0))],
        out_specs=[row(D_MODEL), pl.BlockSpec((D_MODEL, tm), lambda i: (0, i)),
                   pl.BlockSpec((H_P * 2 * N_KEYS, tm), lambda i: (0, i))],
        out_shape=[jax.ShapeDtypeStruct((n, D_MODEL), F32), jax.ShapeDtypeStruct((D_MODEL, n), BF16),
                   jax.ShapeDtypeStruct((H_P * 2 * N_KEYS, n), F32)],
        compiler_params=_cparams(("parallel",)),
        name="merge",
    )(x2d, hn, att, sg, wa, wb, wo, g_ffn.reshape(1, D_MODEL), wpqt, bd)


def _extract_top(s, k, first_only=True):
    rows = lax.broadcasted_iota(I32, s.shape, 0)
    cur, vals = s, []
    for _ in range(k):
        m = jnp.max(cur, axis=0, keepdims=True)
        if first_only:
            first = jnp.min(jnp.where(cur == m, rows, s.shape[0]), axis=0, keepdims=True)
            cur = jnp.where(rows == first, -jnp.inf, cur)
        else:
            cur = jnp.where(cur == m, -jnp.inf, cur)
        vals.append(m)
    return vals, cur


_STAIR = [(a, b) for a in range(P_TOPK) for b in range(P_TOPK) if (a + 1) * (b + 1) <= P_TOPK]


def _peer_prep_kernel(st_ref, s1m_ref, s2m_ref, e1_ref, e2_ref, tau_ref, top_ref, sel_ref):
    tn = st_ref.shape[1]
    taus = []

    def top_keys(rows, slot):
        def run(first_only):
            s = st_ref[rows, :]
            vals, res = _extract_top(s, P_TOPK, first_only)
            picked = jnp.where(res != s, 1.0, 0.0)
            top_ref[slot] = jnp.concatenate(vals, axis=0)
            sel_ref[slot] = picked
            return picked
        picked = run(False)
        npick = jnp.sum(picked, axis=0, keepdims=True)
        bad = jnp.sum(jnp.where(npick == float(P_TOPK), 0.0, 1.0))

        @pl.when(bad > 0.0)
        def _():
            run(True)

    for h in range(H_P):
        r1s = slice((2 * h) * N_KEYS, (2 * h + 1) * N_KEYS)
        r2s = slice((2 * h + 1) * N_KEYS, (2 * h + 2) * N_KEYS)
        top_keys(r1s, 0)
        top_keys(r2s, 1)
        s1, s2 = st_ref[r1s, :], st_ref[r2s, :]
        sel1, sel2 = sel_ref[0] > 0.0, sel_ref[1] > 0.0
        v1 = [top_ref[0, a:a + 1, :] for a in range(P_TOPK)]
        v2 = [top_ref[1, a:a + 1, :] for a in range(P_TOPK)]
        cand = [v1[a] + v2[b] for a, b in _STAIR]
        npad = (-len(cand)) % 8
        cand = jnp.concatenate(cand + [jnp.full((npad, tn), -jnp.inf, F32)], axis=0)
        top, _ = _extract_top(cand, P_TOPK)
        z = jnp.zeros_like(top[0])
        for c in top:
            z = z + jnp.exp(c - top[0])
        taus.append(top[P_TOPK - 1])
        sl = slice(h * N_KEYS, (h + 1) * N_KEYS)
        outs = ((s1m_ref, jnp.where(sel1, s1, -jnp.inf)), (s2m_ref, jnp.where(sel2, s2, -jnp.inf)),
                (e1_ref, jnp.where(sel1, jnp.exp(s1 - v1[0]), 0.0) / z),
                (e2_ref, jnp.where(sel2, jnp.exp(s2 - v2[0]), 0.0)))
        for ref, val in outs:
            for lt in range(tn // LANES):
                ref[lt, sl, :] = val[:, lt * LANES:(lt + 1) * LANES]
    tau = jnp.concatenate(taus, axis=0)
    for lt in range(tn // LANES):
        tau_ref[lt] = tau[:, lt * LANES:(lt + 1) * LANES]


def _peer_prep(st, tn):
    n = st.shape[1]
    assert n % tn == 0 and tn % LANES == 0
    hk = H_P * N_KEYS
    col = lambda r: pl.BlockSpec((tn // LANES, r, LANES), lambda i: (i, 0, 0))
    return pl.pallas_call(
        _peer_prep_kernel,
        grid=(n // tn,),
        in_specs=[pl.BlockSpec((2 * hk, tn), lambda i: (0, i))],
        out_specs=[col(hk), col(hk), col(hk), col(hk), col(H_P)],
        out_shape=[jax.ShapeDtypeStruct((n // LANES, hk, LANES), F32)] * 4
                  + [jax.ShapeDtypeStruct((n // LANES, H_P, LANES), F32)],
        scratch_shapes=[pltpu.VMEM((2, P_TOPK, tn), F32), pltpu.VMEM((2, N_KEYS, tn), F32)],
        compiler_params=_cparams(("parallel",)),
        name="peer_prep",
    )(st)


def _peer_dense_kernel(u_ref, vt_ref, xn2t_ref, s1m_ref, s2m_ref, e1_ref, e2_ref, tau_ref, h1_ref, gf_ref,
                       y_ref, acc_ref, a_ref, yb_ref):
    e = pl.program_id(1)
    te, tn = u_ref.shape[0], xn2t_ref.shape[1]
    nblk, nlt = te // N_KEYS, tn // LANES

    @pl.when(e == 0)
    def _():
        acc_ref[...] = jnp.zeros(acc_ref.shape, F32)

    a = _dot(u_ref[...], xn2t_ref[...])
    for lt in range(nlt):
        a_ref[lt] = a[:, lt * LANES:(lt + 1) * LANES]

    def token_tile(lt, carry):
        def expert_tile(blk, carry2):
            b0 = pl.multiple_of(blk * N_KEYS, N_KEYS)
            a_idx = e * nblk + blk
            w = jnp.zeros((N_KEYS, LANES), F32)
            for h in range(H_P):
                sl = slice(h * N_KEYS, (h + 1) * N_KEYS)
                s1row = s1m_ref[lt, pl.ds(h * N_KEYS + a_idx, 1), :]
                e1row = e1_ref[lt, pl.ds(h * N_KEYS + a_idx, 1), :]
                t = s2m_ref[lt, sl, :] + s1row
                w = w + jnp.where(t >= tau_ref[lt, h:h + 1, :], e2_ref[lt, sl, :], 0.0) * e1row
            a_t = a_ref[lt, pl.ds(b0, N_KEYS), :]
            gl = 0.5 * a_t * (1.0 + lax.erf(a_t * (2.0 ** -0.5)))
            yb_ref[lt, pl.ds(b0, N_KEYS), :] = (w * gl).astype(BF16)
            return carry2

        lax.fori_loop(0, nblk, expert_tile, 0)
        return carry

    lax.fori_loop(0, nlt, token_tile, 0)
    yb = jnp.concatenate([yb_ref[lt] for lt in range(nlt)], axis=1)
    acc_ref[...] += _dot(vt_ref[...], yb)

    @pl.when(e == pl.num_programs(1) - 1)
    def _():
        h2 = h1_ref[...] + acc_ref[...].T
        y_ref[...] = (h2 * lax.rsqrt(jnp.mean(h2 * h2, axis=-1, keepdims=True) + EPS)) * gf_ref[...]


def _peer_dense(ub, vtb, xn2t, s1m, s2m, e1, e2, tau, h1, g_final, tn, te):
    n = xn2t.shape[1]
    ne = ub.shape[0]
    assert n % tn == 0 and ne % te == 0 and tn % LANES == 0 and te % N_KEYS == 0
    hk = H_P * N_KEYS
    col = lambda r: pl.BlockSpec((tn // LANES, r, LANES), lambda i, e: (i, 0, 0))
    row = lambda w: pl.BlockSpec((tn, w), lambda i, e: (i, 0))
    return pl.pallas_call(
        _peer_dense_kernel,
        grid=(n // tn, ne // te),
        in_specs=[pl.BlockSpec((te, D_MODEL), lambda i, e: (e, 0)),
                  pl.BlockSpec((D_MODEL, te), lambda i, e: (0, e)),
                  pl.BlockSpec((D_MODEL, tn), lambda i, e: (0, i)),
                  col(hk), col(hk), col(hk), col(hk), col(H_P), row(D_MODEL),
                  pl.BlockSpec((1, D_MODEL), lambda i, e: (0, 0))],
        out_specs=row(D_MODEL),
        out_shape=jax.ShapeDtypeStruct((n, D_MODEL), F32),
        scratch_shapes=[pltpu.VMEM((D_MODEL, tn), F32), pltpu.VMEM((tn // LANES, te, LANES), F32),
                        pltpu.VMEM((tn // LANES, te, LANES), BF16)],
        compiler_params=_cparams(("parallel", "arbitrary")),
        name="peer_dense",
    )(ub, vtb, xn2t, s1m, s2m, e1, e2, tau, h1, g_final.reshape(1, D_MODEL))


def _row_tile(n, pref):
    t = pref
    while n % t:
        t //= 2
    return t


def _group(x, pos, tables_rows, attend, c0, n0, m0, lw, fw, g_final):
    nb, t, _ = x.shape
    n = nb * t
    g_mix, wp, bias_row, g_mnorm, wa, wb, wo = lw
    g_ffn, wpqt, bd, ub, vtb = fw
    x2d = x.reshape(n, D_MODEL)
    tm = _row_tile(n, 256)
    tables = _rope_tables(pos)
    if tables_rows != t:
        tables = tuple(jnp.tile(a, (tables_rows // t, 1)) for a in tables)
    (mq, mk, mv, mo, aq, ak, av, akb, avb, iq, sg, sm, ik, ik2) = _inproj(x2d, g_mix, wp, bias_row, tables, tm)
    hn, cc, nn, mm = _mlstm(mq, mk, mv, sm, mo, g_mnorm, c0, n0, m0, nb, t)
    att = attend(aq, iq, sm, ak, av, akb, avb, ik, ik2)
    h1, xn2t, st = _merge(x2d, hn, att, sg, wa, wb, wo, g_ffn, wpqt, bd, tm)
    tn = _row_tile(n, 512)
    s1m, s2m, e1, e2, tau = _peer_prep(st, _row_tile(n, 256))
    y = _peer_dense(ub, vtb, xn2t, s1m, s2m, e1, e2, tau, h1, g_final, tn, 512)
    return (y.reshape(nb, t, D_MODEL), ak.reshape(nb, t, H_A, DH_A), av.reshape(nb, t, H_A, DH_A),
            ik.reshape(nb, t, D_IDX), cc, nn, mm)


def kernel(x_prompt, x_sample, cache_k, cache_v, cache_kidx, page_table, state_C, state_n, state_m,
           g_mix, w_in, b_mgate, g_mnorm, w_a, w_b, w_o, g_ffn, w_pq, sub_keys, peer_u, peer_v, g_final):
    depth = w_in.shape[0]
    assert depth == 1, "single-layer step"
    l = 0
    bp, sp, _ = x_prompt.shape
    bd_, td, _ = x_sample.shape
    past = page_table.shape[1] * PAGE_SIZE

    one = lambda a: a.reshape(a.shape[1:])
    bias_row = jnp.zeros((1, LANES), F32).at[0, SM_IP:SM_IW].set(b_mgate[l].astype(F32))
    lw = (g_mix[l], _prep_w_in(one(w_in)), bias_row, g_mnorm[l],
          one(w_a).astype(BF16), one(w_b).astype(BF16), one(w_o).astype(BF16))
    fw = (g_ffn[l], one(w_pq).T.astype(BF16), _prep_subkeys(one(sub_keys)),
          one(peer_u).astype(BF16), one(peer_v).T.astype(BF16))

    def attend_p(aq, iq, sm, ak, av, akb, avb, ik, ik2):
        return _dsa_prompt(aq, iq, sm, akb, avb, ik2, bp, sp)

    def attend_s(aq, iq, sm, ak, av, akb, avb, ik, ik2):
        return _dsa_sample(aq, iq, sm, ak, av, ik, one(cache_k), one(cache_v), one(cache_kidx), page_table, bd_, td)

    zc = jnp.zeros((bp, H_M, DV_M, DK_M), F32)
    zn = jnp.zeros((bp, H_M, DK_M), F32)
    zm = jnp.zeros((bp, H_M), F32)
    yp, kp, vp, kip, cp, np_, mp = _group(x_prompt, jnp.arange(sp), sp, attend_p, zc, zn, zm, lw, fw, g_final)
    ys, ks, vs, kis, cs, ns, ms = _group(x_sample, past + jnp.arange(td), bd_ * td, attend_s,
                                         one(state_C), one(state_n), one(state_m), lw, fw, g_final)
    st = lambda a: a[None]
    return (yp, ys, st(kp), st(vp), st(kip), st(cp), st(np_), st(mp),
            st(ks), st(vs), st(kis), st(cs), st(ns), st(ms))
```

```python
import functools
import math

import jax
import jax.numpy as jnp
from jax import lax
from jax.experimental import pallas as pl
from jax.experimental.pallas import tpu as pltpu

F32 = jnp.float32
BF16 = jnp.bfloat16
I32 = jnp.int32
I16 = jnp.int16

D_MODEL = 1024
PAGE_SIZE = 128
H_M, DK_M, DV_M, M_CHUNK = 4, 128, 128, 64
H_A, DH_A = 4, 128
H_I, D_IDX = 8, 64
TOPK_MAX = 256
Q_BLOCK = 128
ROPE_THETA = 10000.0
N_KEYS = 128
H_P, D_PKEY, P_TOPK = 8, 128, 16
EPS = 1e-6

MIX_SPLITS = (H_M * DK_M, H_M * DK_M, H_M * DV_M, H_M, H_M, H_M * DV_M,
              H_A * DH_A, H_A * DH_A, H_A * DH_A, H_I * D_IDX, D_IDX, H_I, 2 * D_MODEL)

LANES = 128
VMEM_LIMIT_BYTES = 56 * 1024 * 1024
NEG_INF_KEY = -2139095041
INT_MIN = -2147483648
NEG_BIG = -1e30

SM_IK, SM_IP, SM_LF, SM_IW, SM_END = 0, 64, 68, 72, 80
W_MQ, W_MK, W_MV, W_MO, W_AQ, W_AK, W_AV, W_IQ, W_G, W_SM, W_COLS = (
    0, 512, 1024, 1536, 2048, 2560, 3072, 3584, 4096, 6144, 6272)


def _dot(a, b):
    return jnp.dot(a, b, preferred_element_type=F32)


def _dot_nt(a, b):
    return lax.dot_general(a, b, (((1,), (1,)), ((), ())), preferred_element_type=F32)


def _dot_tn(a, b):
    return lax.dot_general(a, b, (((0,), (0,)), ((), ())), preferred_element_type=F32)


def _okey(x):
    i = lax.bitcast_convert_type(x, I32)
    return i ^ ((i >> 31) & jnp.int32(0x7FFFFFFF))


def _cparams(sem):
    return pltpu.CompilerParams(dimension_semantics=sem, vmem_limit_bytes=VMEM_LIMIT_BYTES)


def _inproj_kernel(x_ref, g_ref, w_ref, bias_ref, cosa_ref, sina_ref, cosi_ref, sini_ref,
                   mq_ref, mk_ref, mv_ref, mo_ref, aq_ref, ak_ref, av_ref, akb_ref, avb_ref,
                   iq_ref, sg_ref, sm_ref, ik_ref, ik2_ref):
    x = x_ref[...]
    ms = jnp.mean(x * x, axis=-1, keepdims=True)
    xn = ((x * lax.rsqrt(ms + EPS)) * g_ref[...]).astype(BF16)

    def proj(lo, width):
        return _dot(xn, w_ref[:, lo:lo + width])

    cosa, sina = cosa_ref[...], sina_ref[...]
    cosi, sini = cosi_ref[...], sini_ref[...]
    lane = lax.broadcasted_iota(I32, (x.shape[0], LANES), 1)
    first_half = (lane % 64) < 32

    def rope128(z):
        outs = []
        for h in range(z.shape[1] // LANES):
            zh = z[:, h * LANES:(h + 1) * LANES]
            outs.append(zh * cosa + pltpu.roll(zh, 64, 1) * sina)
        return jnp.concatenate(outs, axis=1) if len(outs) > 1 else outs[0]

    def rope64_slab(zh):
        rot = jnp.where(first_half, pltpu.roll(zh, 96, 1), pltpu.roll(zh, 32, 1))
        return zh * cosi + rot * sini

    mq_ref[...] = (proj(W_MQ, 512) * (DK_M ** -0.5)).astype(BF16)
    mk_ref[...] = proj(W_MK, 512).astype(BF16)
    mv_ref[...] = proj(W_MV, 512).astype(BF16)
    mo_ref[...] = proj(W_MO, 512)
    aq_ref[...] = rope128(proj(W_AQ, 512)).astype(BF16)
    ak = rope128(proj(W_AK, 512))
    ak_ref[...] = ak
    akb_ref[...] = ak.astype(BF16)
    av = proj(W_AV, 512)
    av_ref[...] = av
    avb_ref[...] = av.astype(BF16)
    ziq = proj(W_IQ, 512)
    iq = jnp.concatenate([rope64_slab(ziq[:, h * LANES:(h + 1) * LANES]) for h in range(4)], axis=1)
    iq_ref[...] = (iq * (D_IDX ** -0.5)).astype(BF16)
    sg_ref[...] = jax.nn.sigmoid(proj(W_G, 2 * D_MODEL))
    zs = proj(W_SM, LANES)
    ikr = rope64_slab(zs)
    zb = zs + bias_ref[...]
    logsig = jnp.minimum(zb, 0.0) - jnp.log1p(jnp.exp(-jnp.abs(zb)))
    sm = jnp.where(lane < SM_IP, ikr,
                   jnp.where(lane < SM_LF, zb,
                             jnp.where(lane < SM_IW, logsig,
                                       jnp.where(lane < SM_END, zs * (H_I ** -0.5), 0.0))))
    sm_ref[...] = sm
    ik_ref[...] = ikr[:, :D_IDX]
    ik2_ref[...] = jnp.where(lane < D_IDX, ikr, pltpu.roll(ikr, 64, 1)).astype(BF16)


def _prep_w_in(w_in):
    parts, o = [], 0
    for s in MIX_SPLITS:
        parts.append(w_in[:, o:o + s])
        o += s
    mq, mk, mv, mi, mf, mo, aq, ak, av, iq, ik, iw, gates = parts
    pad = jnp.zeros((w_in.shape[0], LANES - SM_END), w_in.dtype)
    small = jnp.concatenate([ik, mi, mf, iw, pad], axis=1)
    return jnp.concatenate([mq, mk, mv, mo, aq, ak, av, iq, gates, small], axis=1).astype(BF16)


def _rope_tables(pos):
    pos = pos.astype(F32)[:, None]

    def tab(d):
        inv = ROPE_THETA ** (-jnp.arange(0, d, 2, dtype=F32) / d)
        ang = pos * inv[None, :]
        c, s = jnp.cos(ang), jnp.sin(ang)
        cos = jnp.concatenate([c, c], axis=1)
        sin = jnp.concatenate([-s, s], axis=1)
        reps = LANES // d
        return jnp.tile(cos, (1, reps)), jnp.tile(sin, (1, reps))

    cosa, sina = tab(DH_A)
    cosi, sini = tab(D_IDX)
    return cosa, sina, cosi, sini


def _inproj(x2d, g_mix, wp, bias_row, tables, tm):
    n = x2d.shape[0]
    t_tab = tables[0].shape[0]
    assert n % tm == 0 and t_tab % tm == 0
    nt = t_tab // tm
    row = lambda i: (i, 0)
    tabspec = pl.BlockSpec((tm, LANES), lambda i: (i % nt, 0))
    full = lambda shape: pl.BlockSpec(shape, lambda i: (0, 0))
    out_defs = [
        (512, BF16), (512, BF16), (512, BF16), (512, F32), (512, BF16), (512, F32), (512, F32),
        (512, BF16), (512, BF16), (512, BF16), (2 * D_MODEL, F32), (LANES, F32), (D_IDX, F32), (LANES, BF16)]
    return pl.pallas_call(
        _inproj_kernel,
        grid=(n // tm,),
        in_specs=[pl.BlockSpec((tm, D_MODEL), row), full((1, D_MODEL)), full((D_MODEL, W_COLS)),
                  full((1, LANES)), tabspec, tabspec, tabspec, tabspec],
        out_specs=[pl.BlockSpec((tm, w), row) for w, _ in out_defs],
        out_shape=[jax.ShapeDtypeStruct((n, w), dt) for w, dt in out_defs],
        compiler_params=_cparams(("parallel",)),
        name="inproj",
    )(x2d, g_mix.reshape(1, D_MODEL), wp, bias_row, *tables)


def _mlstm_kernel(q_ref, k_ref, v_ref, sm_ref, mo_ref, gm_ref, c0_ref, n0_ref, m0_ref,
                  hn_ref, c_ref, n_ref, m_ref, *, c, nchunks):
    @pl.when(pl.program_id(1) == 0)
    def _():
        c_ref[...] = c0_ref[...]
        n_ref[...] = n0_ref[...]
        m_ref[...] = m0_ref[...]

    ri = lax.broadcasted_iota(I32, (c, c), 0)
    ci = lax.broadcasted_iota(I32, (c, c), 1)
    eye = ri == ci
    tri = ri >= ci

    def to_row(col):
        return jnp.sum(jnp.where(eye, jnp.broadcast_to(col, (c, c)), 0.0), axis=0, keepdims=True)

    def chunk(rows):
        q = q_ref[0, rows, :]
        k = k_ref[0, rows, :]
        v = v_ref[0, rows, :]
        sm = sm_ref[0, rows, :]
        mo = mo_ref[0, rows, :]
        hs = range(H_M)
        sls = [slice(h * LANES, (h + 1) * LANES) for h in hs]
        state = [(m_ref[0, h][:, :1], n_ref[0, h], c_ref[0, h]) for h in hs]
        qk = [_dot_nt(q[:, sl], k[:, sl]) for sl in sls]
        qc = [_dot_nt(q[:, sls[h]], state[h][2].astype(BF16)) for h in hs]
        gate = []
        for h in hs:
            ipc = sm[:, SM_IP + h:SM_IP + h + 1]
            lfc = sm[:, SM_LF + h:SM_LF + h + 1]
            lf_row = to_row(lfc)
            ip_row = to_row(ipc)
            b_col = jnp.sum(jnp.where(tri, jnp.broadcast_to(lf_row, (c, c)), 0.0), axis=1, keepdims=True)
            b_row = to_row(b_col)
            m_prev = state[h][0]
            m_inter = b_col + m_prev
            dmat = jnp.where(tri, b_col - b_row + ip_row, -jnp.inf)
            m_t = jnp.maximum(m_inter, jnp.max(dmat, axis=1, keepdims=True))
            m_new = m_t[c - 1:c, :]
            b_last = b_col[c - 1:c, :]
            gate.append(dict(m_t=m_t, w_inter=jnp.exp(m_inter - m_t), decay=jnp.exp(dmat - m_t), m_new=m_new,
                             g_state=jnp.exp(b_last + m_prev - m_new),
                             g_tok=jnp.exp(b_last - b_col + ipc - m_new)))
        s_mat = [qk[h] * gate[h]["decay"] for h in hs]
        sv = [_dot(s_mat[h].astype(BF16), v[:, sls[h]]) for h in hs]
        upd = [_dot_tn((gate[h]["g_tok"] * v[:, sls[h]].astype(F32)).astype(BF16), k[:, sls[h]]) for h in hs]
        for h in hs:
            g, sl = gate[h], sls[h]
            _, n_prev, c_prev = state[h]
            num = g["w_inter"] * qc[h] + sv[h]
            den = (g["w_inter"] * jnp.sum(q[:, sl].astype(F32) * n_prev, axis=1, keepdims=True)
                   + jnp.sum(s_mat[h], axis=1, keepdims=True))
            hh = num / jnp.maximum(jnp.abs(den), jnp.exp(-g["m_t"]))
            y = hh * lax.rsqrt(jnp.mean(hh * hh, axis=1, keepdims=True) + EPS) * gm_ref[:, sl]
            hn_ref[0, rows, sl] = (jax.nn.sigmoid(mo[:, sl]) * y).astype(BF16)
            c_ref[0, h] = g["g_state"] * c_prev + upd[h]
            n_ref[0, h] = (g["g_state"] * n_prev
                           + jnp.sum(g["g_tok"] * k[:, sl].astype(F32), axis=0, keepdims=True))
            m_ref[0, h] = jnp.broadcast_to(g["m_new"], (1, LANES))

    if nchunks == 1:
        chunk(slice(0, c))
    else:
        def body(i, carry):
            chunk(pl.ds(pl.multiple_of(i * c, c), c))
            return carry
        lax.fori_loop(0, nchunks, body, 0)


def _mlstm(mq, mk, mv, sm, mo, g_mnorm, c0, n0, m0, nb, t):
    c = math.gcd(M_CHUNK, t)
    tb = min(t, 8 * c)
    assert t % tb == 0
    r3 = lambda a: a.reshape(nb, t, a.shape[-1])
    n0b = n0.reshape(nb, H_M, 1, DK_M).astype(F32)
    m0b = jnp.broadcast_to(m0.astype(F32)[:, :, None, None], (nb, H_M, 1, LANES))
    tok = lambda w: pl.BlockSpec((1, tb, w), lambda b, j: (b, j, 0))
    st4 = lambda shape: pl.BlockSpec(shape, lambda b, j: (b, 0, 0, 0))
    hn, cc, nn, mm = pl.pallas_call(
        functools.partial(_mlstm_kernel, c=c, nchunks=tb // c),
        grid=(nb, t // tb),
        in_specs=[tok(512), tok(512), tok(512), tok(LANES), tok(512),
                  pl.BlockSpec((1, H_M * DV_M), lambda b, j: (0, 0)),
                  st4((1, H_M, DV_M, DK_M)), st4((1, H_M, 1, DK_M)), st4((1, H_M, 1, LANES))],
        out_specs=[tok(512), st4((1, H_M, DV_M, DK_M)), st4((1, H_M, 1, DK_M)), st4((1, H_M, 1, LANES))],
        out_shape=[jax.ShapeDtypeStruct((nb, t, H_M * DV_M), BF16),
                   jax.ShapeDtypeStruct((nb, H_M, DV_M, DK_M), F32),
                   jax.ShapeDtypeStruct((nb, H_M, 1, DK_M), F32),
                   jax.ShapeDtypeStruct((nb, H_M, 1, LANES), F32)],
        compiler_params=_cparams(("parallel", "arbitrary")),
        name="mlstm",
    )(r3(mq), r3(mk), r3(mv), r3(sm), r3(mo), g_mnorm.reshape(1, H_M * DV_M),
      c0.astype(F32), n0b, m0b)
    return hn.reshape(nb * t, H_M * DV_M), cc, nn.reshape(nb, H_M, DK_M), mm[:, :, 0, 0]


def _kth_largest_key(count_ge, k, shape):
    kf = jnp.float32(k)
    r0 = jnp.where(count_ge(jnp.zeros(shape, I32)) >= kf, jnp.int32(0), jnp.int32(INT_MIN))

    def body(t, r):
        cand = r + lax.shift_left(jnp.int32(1), jnp.int32(30) - t)
        return jnp.where(count_ge(cand) >= kf, cand, r)

    return lax.fori_loop(0, 31, body, r0)


def _dsa_prompt_kernel(aq_ref, iq_ref, sm_ref, kb_ref, vb_ref, ik2_ref, out_ref, skey_ref, acc_ref, mx_ref, l_ref,
                       hi_ref, lo_ref, *, kc, topk):
    i = pl.program_id(1)
    qb = Q_BLOCK
    nk = (i * qb + qb + kc - 1) // kc
    iq = iq_ref[0]
    lane = lax.broadcasted_iota(I32, (qb, LANES), 1)
    zero = jnp.zeros((qb, LANES), BF16)
    lhs = []
    for hp in range(H_I // 2):
        slab = iq[:, hp * LANES:(hp + 1) * LANES]
        lhs.append(jnp.where(lane < D_IDX, slab, zero))
        lhs.append(jnp.where(lane >= D_IDX, slab, zero))
    sm = sm_ref[0]
    qpos = i * qb + lax.broadcasted_iota(I32, (qb, 1), 0)

    def score_body(cidx, carry):
        k0 = pl.multiple_of(cidx * kc, kc)
        kk = ik2_ref[0, pl.ds(k0, kc), :]
        prods = [_dot_nt(lhs[h], kk) for h in range(H_I)]
        sc = jnp.zeros((qb, kc), F32)
        for h in range(H_I):
            sc = sc + sm[:, SM_IW + h:SM_IW + h + 1] * jnp.maximum(prods[h], 0.0)
        kpos = k0 + lax.broadcasted_iota(I32, (1, kc), 1)
        sc = jnp.where(kpos <= qpos, sc, -jnp.inf)
        key = _okey(sc)
        skey_ref[:, pl.ds(k0, kc)] = key
        hi_ref[:, pl.ds(k0, kc)] = (key >> 16).astype(I16)
        lo_ref[:, pl.ds(k0, kc)] = ((key & 0xFFFF) - 32768).astype(I16)
        return carry

    lax.fori_loop(0, nk, score_body, 0)

    def count_ge16(ref, cand):
        c16 = cand.astype(I16)
        one, zero = jnp.ones((), I16), jnp.zeros((), I16)

        def body(cidx, acc):
            k0 = pl.multiple_of(cidx * kc, kc)
            m = jnp.where(ref[:, pl.ds(k0, kc)] >= c16, one, zero)
            part = m[:, :LANES]
            for t in range(1, kc // LANES):
                part = part + m[:, t * LANES:(t + 1) * LANES]
            return acc + part
        acc = lax.fori_loop(0, nk, body, jnp.zeros((qb, LANES), I16))
        return jnp.sum(acc.astype(F32), axis=1, keepdims=True)

    def kth16(ref, kf):
        r0 = jnp.where(count_ge16(ref, jnp.zeros((qb, 1), I32)) >= kf, jnp.int32(0), jnp.int32(-32768))

        def body(t, r):
            cand = r + lax.shift_left(jnp.int32(1), jnp.int32(14) - t)
            return jnp.where(count_ge16(ref, cand) >= kf, cand, r)

        return lax.fori_loop(0, 15, body, r0)

    kf = jnp.full((qb, 1), float(topk), F32)
    p_hi = kth16(hi_ref, kf)
    above = jnp.where(p_hi < 32767, count_ge16(hi_ref, jnp.minimum(p_hi + 1, 32767)), 0.0)
    p16 = p_hi.astype(I16)

    def bucket_body(cidx, carry):
        k0 = pl.multiple_of(cidx * kc, kc)
        inb = hi_ref[:, pl.ds(k0, kc)] == p16
        lo_ref[:, pl.ds(k0, kc)] = jnp.where(inb, lo_ref[:, pl.ds(k0, kc)], jnp.full((), -32768, I16))
        return carry

    lax.fori_loop(0, nk, bucket_body, 0)
    p_lo = kth16(lo_ref, kf - above)
    thr = lax.shift_left(p_hi, jnp.int32(16)) | (p_lo + 32768)
    thr = jnp.maximum(thr, jnp.int32(NEG_INF_KEY + 1))
    scale = DH_A ** -0.5
    aq = aq_ref[0]
    ntile = kc // LANES
    heads = [slice(h * LANES, (h + 1) * LANES) for h in range(H_A)]

    def fold_tiles(x, op):
        r = x[:, :LANES]
        for u in range(1, ntile):
            r = op(r, x[:, u * LANES:(u + 1) * LANES])
        return r

    mx_ref[...] = jnp.full(mx_ref.shape, NEG_BIG, F32)
    l_ref[...] = jnp.zeros(l_ref.shape, F32)
    acc_ref[...] = jnp.zeros(acc_ref.shape, F32)

    def bias_body(cidx, carry):
        k0 = pl.multiple_of(cidx * kc, kc)
        bias = jnp.where(skey_ref[:, pl.ds(k0, kc)] >= thr, 0.0, NEG_BIG)
        skey_ref[:, pl.ds(k0, kc)] = lax.bitcast_convert_type(bias, I32)
        return carry

    lax.fori_loop(0, nk, bias_body, 0)
    load_bias = lambda k0: lax.bitcast_convert_type(skey_ref[:, pl.ds(k0, kc)], F32)

    def max_body(cidx, carry):
        k0 = pl.multiple_of(cidx * kc, kc)
        bias = load_bias(k0)
        lgs = [_dot_nt(aq[:, sl], kb_ref[0, pl.ds(k0, kc), sl]) for sl in heads]
        for h in range(H_A):
            mx_ref[h] = jnp.maximum(mx_ref[h], fold_tiles(lgs[h] + bias, jnp.maximum))
        return carry

    lax.fori_loop(0, nk, max_body, 0)
    for h in range(H_A):
        mx_ref[h] = jnp.broadcast_to(jnp.max(mx_ref[h], axis=1, keepdims=True) * scale, (qb, LANES))

    def sum_body(cidx, carry):
        k0 = pl.multiple_of(cidx * kc, kc)
        bias = load_bias(k0)
        lgs = [_dot_nt(aq[:, sl], kb_ref[0, pl.ds(k0, kc), sl]) for sl in heads]
        ps = []
        for h in range(H_A):
            lg = lgs[h] * scale + bias
            p = jnp.exp(lg - jnp.concatenate([mx_ref[h]] * ntile, axis=1))
            l_ref[h] += fold_tiles(p, jnp.add)
            ps.append(p.astype(BF16))
        for h, sl in enumerate(heads):
            acc_ref[h] += _dot(ps[h], vb_ref[0, pl.ds(k0, kc), sl])
        return carry

    lax.fori_loop(0, nk, sum_body, 0)
    for h, sl in enumerate(heads):
        out_ref[0, :, sl] = (acc_ref[h] / jnp.sum(l_ref[h], axis=1, keepdims=True)).astype(BF16)


def _dsa_prompt(aq, iq, sm, akb, avb, ik2, nb, s):
    topk = min(TOPK_MAX, s // 4)
    kc = min(512, s)
    assert s % kc == 0 and s % Q_BLOCK == 0 and kc % Q_BLOCK == 0
    r3 = lambda a: a.reshape(nb, s, a.shape[-1])
    qspec = lambda w: pl.BlockSpec((1, Q_BLOCK, w), lambda b, i: (b, i, 0))
    kspec = lambda w: pl.BlockSpec((1, s, w), lambda b, i: (b, 0, 0))
    out = pl.pallas_call(
        functools.partial(_dsa_prompt_kernel, kc=kc, topk=topk),
        grid=(nb, s // Q_BLOCK),
        in_specs=[qspec(512), qspec(512), qspec(LANES), kspec(512), kspec(512), kspec(LANES)],
        out_specs=qspec(512),
        out_shape=jax.ShapeDtypeStruct((nb, s, H_A * DH_A), BF16),
        scratch_shapes=[pltpu.VMEM((Q_BLOCK, s), I32)] + [pltpu.VMEM((H_A, Q_BLOCK, LANES), F32)] * 3
                       + [pltpu.VMEM((Q_BLOCK, s), I16)] * 2,
        compiler_params=_cparams(("parallel", "arbitrary")),
        name="dsa_prompt",
    )(r3(aq), r3(iq), r3(sm), r3(akb), r3(avb), r3(ik2))
    return out.reshape(nb * s, H_A * DH_A)


def _idx_lhs(iq_ref):
    iqf = iq_ref[0].astype(F32)
    return jnp.concatenate([iqf[:, h * D_IDX:(h + 1) * D_IDX] for h in range(H_I)], axis=0).astype(BF16)


def _idx_score(r, sm, t):
    sc = jnp.zeros((t, r.shape[1]), F32)
    for h in range(H_I):
        sc = sc + sm[:, SM_IW + h:SM_IW + h + 1] * jnp.maximum(r[h * t:(h + 1) * t, :], 0.0)
    return sc


def _dsa_s_score_kernel(pt_ref, iq_ref, sm_ref, ikn_ref, *rest, pgs, t, topk, npast):
    pages = rest[:pgs]
    skey_ref, skn_ref, thr_ref = rest[pgs:]
    j = pl.program_id(1)
    lhs = _idx_lhs(iq_ref)
    sm = sm_ref[0]
    kall = jnp.concatenate([pages[r][0] for r in range(pgs)], axis=0).astype(BF16)
    base = pl.multiple_of(j * (pgs * PAGE_SIZE), pgs * PAGE_SIZE)
    skey_ref[0, :, pl.ds(base, pgs * PAGE_SIZE)] = _okey(_idx_score(_dot_nt(lhs, kall), sm, t))

    @pl.when(j == pl.num_programs(1) - 1)
    def _():
        scn = _idx_score(_dot_nt(lhs, ikn_ref[0]), sm, t)
        ti = lax.broadcasted_iota(I32, (t, LANES), 0)
        si = lax.broadcasted_iota(I32, (t, LANES), 1)
        kn = _okey(jnp.where(si <= ti, scn, -jnp.inf))
        skn_ref[0] = kn
        cw = 1024 if npast % 1024 == 0 else PAGE_SIZE

        def count_ge(cand):
            def body(cidx, acc):
                k0 = pl.multiple_of(cidx * cw, cw)
                m = jnp.where(skey_ref[0, :, pl.ds(k0, cw)] >= cand, 1.0, 0.0)
                part = m[:, :LANES]
                for u in range(1, cw // LANES):
                    part = part + m[:, u * LANES:(u + 1) * LANES]
                return acc + part
            acc = lax.fori_loop(0, npast // cw, body, jnp.where(kn >= cand, 1.0, 0.0))
            return jnp.sum(acc, axis=1, keepdims=True)

        thr = _kth_largest_key(count_ge, topk, (t, 1))
        thr = jnp.maximum(thr, jnp.int32(NEG_INF_KEY + 1))
        thr_ref[0] = jnp.broadcast_to(thr, (t, LANES))


def _dsa_s_att_kernel(pt_ref, aq_ref, skey_ref, skn_ref, thr_ref, kn_ref, vn_ref, emat_ref, *rest, pg, t):
    kpages, vpages = rest[:pg], rest[pg:2 * pg]
    out_ref, m_sc, l_sc, acc_sc = rest[2 * pg:]
    j = pl.program_id(1)

    @pl.when(j == 0)
    def _():
        m_sc[...] = jnp.full(m_sc.shape, NEG_BIG, F32)
        l_sc[...] = jnp.zeros(l_sc.shape, F32)
        acc_sc[...] = jnp.zeros(acc_sc.shape, F32)

    scale = DH_A ** -0.5
    aqf = aq_ref[0].astype(F32)
    qs = jnp.concatenate([aqf[:, h * LANES:(h + 1) * LANES] for h in range(H_A)], axis=0).astype(BF16)
    thr = thr_ref[0][:, :1]
    rows = lambda a, h: a[h * t:(h + 1) * t]

    def update(lg, sel, pv_of):
        lg = jnp.where(sel, lg * scale, NEG_BIG)
        m_i, l_i = m_sc[:, :1], l_sc[:, :1]
        m_new = jnp.maximum(m_i, jnp.max(lg, axis=1, keepdims=True))
        alpha = jnp.exp(m_i - m_new)
        p = jnp.where(sel, jnp.exp(lg - m_new), 0.0)
        l_new = alpha * l_i + jnp.sum(p, axis=1, keepdims=True)
        acc_sc[...] = alpha * acc_sc[...] + pv_of(p.astype(BF16))
        m_sc[...] = jnp.broadcast_to(m_new, m_sc.shape)
        l_sc[...] = jnp.broadcast_to(l_new, l_sc.shape)

    ncol = pg * PAGE_SIZE * H_A
    kall = jnp.concatenate([kpages[r][0] for r in range(pg)], axis=0).astype(BF16)
    vall = jnp.concatenate([vpages[r][0] for r in range(pg)], axis=0).astype(BF16)
    picked = jnp.where(skey_ref[0] >= thr, 1.0, 0.0)
    picked = jnp.concatenate([picked] * H_A, axis=0).astype(BF16)
    spread = jnp.concatenate([_dot(picked[:, r * PAGE_SIZE:(r + 1) * PAGE_SIZE], emat_ref[...])
                              for r in range(pg)], axis=1)
    col_head = lax.broadcasted_iota(I32, (H_A * t, ncol), 1) & (H_A - 1)
    row_head = lax.shift_right_logical(lax.broadcasted_iota(I32, (H_A * t, ncol), 0), int(math.log2(t)))
    sel_past = jnp.where(col_head == row_head, spread, 0.0) > 0.5
    update(_dot_nt(qs, kall), sel_past, lambda pb: _dot(pb, vall))

    @pl.when(j == pl.num_programs(1) - 1)
    def _():
        sel1 = jnp.where(skn_ref[0] >= thr, 1, 0)
        sel_new = jnp.concatenate([sel1] * H_A, axis=0) > 0
        hs = lambda ref, h: ref[0, :, h * LANES:(h + 1) * LANES]
        lg = jnp.concatenate([rows(_dot_nt(qs, hs(kn_ref, h)), h) for h in range(H_A)], axis=0)
        update(lg, sel_new,
               lambda pb: jnp.concatenate([rows(_dot(pb, hs(vn_ref, h)), h) for h in range(H_A)], axis=0))
        o = acc_sc[...] / l_sc[:, :1]
        for h in range(H_A):
            out_ref[0, :, h * LANES:(h + 1) * LANES] = rows(o, h).astype(BF16)


def _dsa_sample(aq, iq, sm, ak, av, ik, cache_k, cache_v, cache_kidx, page_table, nb, t):
    n_pages = page_table.shape[1]
    npast = n_pages * PAGE_SIZE
    topk = min(TOPK_MAX, (npast + t) // 4)
    pg = math.gcd(8, n_pages)
    pgs = math.gcd(32, n_pages)
    assert t & (t - 1) == 0 and H_A & (H_A - 1) == 0
    n_pool = cache_k.shape[0]
    ck = cache_k.reshape(n_pool, PAGE_SIZE * H_A, DH_A)
    cv = cache_v.reshape(n_pool, PAGE_SIZE * H_A, DH_A)
    emat = (jnp.arange(PAGE_SIZE * H_A)[None, :] // H_A == jnp.arange(PAGE_SIZE)[:, None]).astype(BF16)
    r3 = lambda a: a.reshape(nb, t, a.shape[-1])
    padrows = lambda a: jnp.pad(r3(a), ((0, 0), (0, LANES - t), (0, 0))).astype(BF16)
    tokspec = lambda w: pl.BlockSpec((1, t, w), lambda b, j, pt: (b, 0, 0))
    newspec = lambda w: pl.BlockSpec((1, LANES, w), lambda b, j, pt: (b, 0, 0))

    def pagespec(rows_, w, r, per_step):
        return pl.BlockSpec((1, rows_, w), lambda b, j, pt, r=r: (pt[b, j * per_step + r], 0, 0))

    skey, skn, thr = pl.pallas_call(
        functools.partial(_dsa_s_score_kernel, pgs=pgs, t=t, topk=topk, npast=npast),
        grid_spec=pltpu.PrefetchScalarGridSpec(
            num_scalar_prefetch=1,
            grid=(nb, n_pages // pgs),
            in_specs=[tokspec(512), tokspec(LANES), newspec(D_IDX)]
                     + [pagespec(PAGE_SIZE, D_IDX, r, pgs) for r in range(pgs)],
            out_specs=[tokspec(npast), tokspec(LANES), tokspec(LANES)],
        ),
        out_shape=[jax.ShapeDtypeStruct((nb, t, npast), I32),
                   jax.ShapeDtypeStruct((nb, t, LANES), I32),
                   jax.ShapeDtypeStruct((nb, t, LANES), I32)],
        compiler_params=_cparams(("parallel", "arbitrary")),
        name="dsa_sample_score",
    )(page_table, r3(iq), r3(sm), padrows(ik), *([cache_kidx] * pgs))

    att = pl.pallas_call(
        functools.partial(_dsa_s_att_kernel, pg=pg, t=t),
        grid_spec=pltpu.PrefetchScalarGridSpec(
            num_scalar_prefetch=1,
            grid=(nb, n_pages // pg),
            in_specs=[tokspec(512),
                      pl.BlockSpec((1, t, pg * PAGE_SIZE), lambda b, j, pt: (b, 0, j)),
                      tokspec(LANES), tokspec(LANES), newspec(512), newspec(512),
                      pl.BlockSpec(emat.shape, lambda b, j, pt: (0, 0))]
                     + [pagespec(PAGE_SIZE * H_A, DH_A, r, pg) for r in range(pg)] * 2,
            out_specs=tokspec(512),
            scratch_shapes=[pltpu.VMEM((H_A * t, LANES), F32)] * 3,
        ),
        out_shape=jax.ShapeDtypeStruct((nb, t, H_A * DH_A), BF16),
        compiler_params=_cparams(("parallel", "arbitrary")),
        name="dsa_sample_att",
    )(page_table, r3(aq), skey, skn, thr, padrows(ak), padrows(av), emat, *([ck] * pg), *([cv] * pg))
    return att.reshape(nb * t, H_A * DH_A)


def _merge_kernel(x_ref, hn_ref, att_ref, sg_ref, wa_ref, wb_ref, wo_ref, gf_ref, wpqt_ref, bd_ref,
                  h1_ref, xn2t_ref, st_ref):
    sg = sg_ref[...]
    merged = sg[:, :D_MODEL] * _dot(hn_ref[...], wa_ref[...]) + sg[:, D_MODEL:] * _dot(att_ref[...], wb_ref[...])
    h1 = x_ref[...] + _dot(merged.astype(BF16), wo_ref[...])
    h1_ref[...] = h1
    xn2 = (h1 * lax.rsqrt(jnp.mean(h1 * h1, axis=-1, keepdims=True) + EPS)) * gf_ref[...]
    xn2t = xn2.T.astype(BF16)
    xn2t_ref[...] = xn2t
    qt = _dot(wpqt_ref[...], xn2t).astype(BF16)
    for h in range(H_P):
        st_ref[h * 2 * N_KEYS:(h + 1) * 2 * N_KEYS, :] = _dot(bd_ref[h], qt[h * D_PKEY:(h + 1) * D_PKEY, :])


def _prep_subkeys(sub_keys):
    z = jnp.zeros_like(sub_keys[:, 0])
    top = jnp.concatenate([sub_keys[:, 0], z], axis=-1)
    bot = jnp.concatenate([z, sub_keys[:, 1]], axis=-1)
    return jnp.concatenate([top, bot], axis=1).astype(BF16)


def _merge(x2d, hn, att, sg, wa, wb, wo, g_ffn, wpqt, bd, tm):
    n = x2d.shape[0]
    assert n % tm == 0
    row = lambda w: pl.BlockSpec((tm, w), lambda i: (i, 0))
    full2 = lambda a: pl.BlockSpec(a.shape, lambda i: (0, 0))
    return pl.pallas_call(
        _merge_kernel,
        grid=(n // tm,),
        in_specs=[row(D_MODEL), row(512), row(512), row(2 * D_MODEL), full2(wa), full2(wb), full2(wo),
                  pl.BlockSpec((1, D_MODEL), lambda i: (0, 0)), full2(wpqt),
                  pl.BlockSpec(bd.shape, lambda i: (0, 0, 0))],
        out_specs=[row(D_MODEL), pl.BlockSpec((D_MODEL, tm), lambda i: (0, i)),
                   pl.BlockSpec((H_P * 2 * N_KEYS, tm), lambda i: (0, i))],
        out_shape=[jax.ShapeDtypeStruct((n, D_MODEL), F32), jax.ShapeDtypeStruct((D_MODEL, n), BF16),
                   jax.ShapeDtypeStruct((H_P * 2 * N_KEYS, n), F32)],
        compiler_params=_cparams(("parallel",)),
        name="merge",
    )(x2d, hn, att, sg, wa, wb, wo, g_ffn.reshape(1, D_MODEL), wpqt, bd)


def _extract_top(s, k, first_only=True):
    rows = lax.broadcasted_iota(I32, s.shape, 0)
    cur, vals = s, []
    for _ in range(k):
        m = jnp.max(cur, axis=0, keepdims=True)
        if first_only:
            first = jnp.min(jnp.where(cur == m, rows, s.shape[0]), axis=0, keepdims=True)
            cur = jnp.where(rows == first, -jnp.inf, cur)
        else:
            cur = jnp.where(cur == m, -jnp.inf, cur)
        vals.append(m)
    return vals, cur


_STAIR = [(a, b) for a in range(P_TOPK) for b in range(P_TOPK) if (a + 1) * (b + 1) <= P_TOPK]


def _peer_prep_kernel(st_ref, s1m_ref, s2m_ref, e1_ref, e2_ref, tau_ref, top_ref, sel_ref):
    tn = st_ref.shape[1]
    taus = []

    def top_keys(rows, slot):
        def run(first_only):
            s = st_ref[rows, :]
            vals, res = _extract_top(s, P_TOPK, first_only)
            picked = jnp.where(res != s, 1.0, 0.0)
            top_ref[slot] = jnp.concatenate(vals, axis=0)
            sel_ref[slot] = picked
            return picked
        picked = run(False)
        npick = jnp.sum(picked, axis=0, keepdims=True)
        bad = jnp.sum(jnp.where(npick == float(P_TOPK), 0.0, 1.0))

        @pl.when(bad > 0.0)
        def _():
            run(True)

    for h in range(H_P):
        r1s = slice((2 * h) * N_KEYS, (2 * h + 1) * N_KEYS)
        r2s = slice((2 * h + 1) * N_KEYS, (2 * h + 2) * N_KEYS)
        top_keys(r1s, 0)
        top_keys(r2s, 1)
        s1, s2 = st_ref[r1s, :], st_ref[r2s, :]
        sel1, sel2 = sel_ref[0] > 0.0, sel_ref[1] > 0.0
        v1 = [top_ref[0, a:a + 1, :] for a in range(P_TOPK)]
        v2 = [top_ref[1, a:a + 1, :] for a in range(P_TOPK)]
        cand = [v1[a] + v2[b] for a, b in _STAIR]
        npad = (-len(cand)) % 8
        cand = jnp.concatenate(cand + [jnp.full((npad, tn), -jnp.inf, F32)], axis=0)
        top, _ = _extract_top(cand, P_TOPK)
        z = jnp.zeros_like(top[0])
        for c in top:
            z = z + jnp.exp(c - top[0])
        taus.append(top[P_TOPK - 1])
        sl = slice(h * N_KEYS, (h + 1) * N_KEYS)
        outs = ((s1m_ref, jnp.where(sel1, s1, -jnp.inf)), (s2m_ref, jnp.where(sel2, s2, -jnp.inf)),
                (e1_ref, jnp.where(sel1, jnp.exp(s1 - v1[0]), 0.0) / z),
                (e2_ref, jnp.where(sel2, jnp.exp(s2 - v2[0]), 0.0)))
        for ref, val in outs:
            for lt in range(tn // LANES):
                ref[lt, sl, :] = val[:, lt * LANES:(lt + 1) * LANES]
    tau = jnp.concatenate(taus, axis=0)
    for lt in range(tn // LANES):
        tau_ref[lt] = tau[:, lt * LANES:(lt + 1) * LANES]


def _peer_prep(st, tn):
    n = st.shape[1]
    assert n % tn == 0 and tn % LANES == 0
    hk = H_P * N_KEYS
    col = lambda r: pl.BlockSpec((tn // LANES, r, LANES), lambda i: (i, 0, 0))
    return pl.pallas_call(
        _peer_prep_kernel,
        grid=(n // tn,),
        in_specs=[pl.BlockSpec((2 * hk, tn), lambda i: (0, i))],
        out_specs=[col(hk), col(hk), col(hk), col(hk), col(H_P)],
        out_shape=[jax.ShapeDtypeStruct((n // LANES, hk, LANES), F32)] * 4
                  + [jax.ShapeDtypeStruct((n // LANES, H_P, LANES), F32)],
        scratch_shapes=[pltpu.VMEM((2, P_TOPK, tn), F32), pltpu.VMEM((2, N_KEYS, tn), F32)],
        compiler_params=_cparams(("parallel",)),
        name="peer_prep",
    )(st)


def _peer_dense_kernel(u_ref, vt_ref, xn2t_ref, s1m_ref, s2m_ref, e1_ref, e2_ref, tau_ref, h1_ref, gf_ref,
                       y_ref, acc_ref, a_ref, yb_ref):
    e = pl.program_id(1)
    te, tn = u_ref.shape[0], xn2t_ref.shape[1]
    nblk, nlt = te // N_KEYS, tn // LANES

    @pl.when(e == 0)
    def _():
        acc_ref[...] = jnp.zeros(acc_ref.shape, F32)

    a = _dot(u_ref[...], xn2t_ref[...])
    for lt in range(nlt):
        a_ref[lt] = a[:, lt * LANES:(lt + 1) * LANES]

    def token_tile(lt, carry):
        def expert_tile(blk, carry2):
            b0 = pl.multiple_of(blk * N_KEYS, N_KEYS)
            a_idx = e * nblk + blk
            w = jnp.zeros((N_KEYS, LANES), F32)
            for h in range(H_P):
                sl = slice(h * N_KEYS, (h + 1) * N_KEYS)
                s1row = s1m_ref[lt, pl.ds(h * N_KEYS + a_idx, 1), :]
                e1row = e1_ref[lt, pl.ds(h * N_KEYS + a_idx, 1), :]
                t = s2m_ref[lt, sl, :] + s1row
                w = w + jnp.where(t >= tau_ref[lt, h:h + 1, :], e2_ref[lt, sl, :], 0.0) * e1row
            a_t = a_ref[lt, pl.ds(b0, N_KEYS), :]
            gl = 0.5 * a_t * (1.0 + lax.erf(a_t * (2.0 ** -0.5)))
            yb_ref[lt, pl.ds(b0, N_KEYS), :] = (w * gl).astype(BF16)
            return carry2

        lax.fori_loop(0, nblk, expert_tile, 0)
        return carry

    lax.fori_loop(0, nlt, token_tile, 0)
    yb = jnp.concatenate([yb_ref[lt] for lt in range(nlt)], axis=1)
    acc_ref[...] += _dot(vt_ref[...], yb)

    @pl.when(e == pl.num_programs(1) - 1)
    def _():
        h2 = h1_ref[...] + acc_ref[...].T
        y_ref[...] = (h2 * lax.rsqrt(jnp.mean(h2 * h2, axis=-1, keepdims=True) + EPS)) * gf_ref[...]


def _peer_dense(ub, vtb, xn2t, s1m, s2m, e1, e2, tau, h1, g_final, tn, te):
    n = xn2t.shape[1]
    ne = ub.shape[0]
    assert n % tn == 0 and ne % te == 0 and tn % LANES == 0 and te % N_KEYS == 0
    hk = H_P * N_KEYS
    col = lambda r: pl.BlockSpec((tn // LANES, r, LANES), lambda i, e: (i, 0, 0))
    row = lambda w: pl.BlockSpec((tn, w), lambda i, e: (i, 0))
    return pl.pallas_call(
        _peer_dense_kernel,
        grid=(n // tn, ne // te),
        in_specs=[pl.BlockSpec((te, D_MODEL), lambda i, e: (e, 0)),
                  pl.BlockSpec((D_MODEL, te), lambda i, e: (0, e)),
                  pl.BlockSpec((D_MODEL, tn), lambda i, e: (0, i)),
                  col(hk), col(hk), col(hk), col(hk), col(H_P), row(D_MODEL),
                  pl.BlockSpec((1, D_MODEL), lambda i, e: (0, 0))],
        out_specs=row(D_MODEL),
        out_shape=jax.ShapeDtypeStruct((n, D_MODEL), F32),
        scratch_shapes=[pltpu.VMEM((D_MODEL, tn), F32), pltpu.VMEM((tn // LANES, te, LANES), F32),
                        pltpu.VMEM((tn // LANES, te, LANES), BF16)],
        compiler_params=_cparams(("parallel", "arbitrary")),
        name="peer_dense",
    )(ub, vtb, xn2t, s1m, s2m, e1, e2, tau, h1, g_final.reshape(1, D_MODEL))


def _row_tile(n, pref):
    t = pref
    while n % t:
        t //= 2
    return t


def _group(x, pos, tables_rows, attend, c0, n0, m0, lw, fw, g_final):
    nb, t, _ = x.shape
    n = nb * t
    g_mix, wp, bias_row, g_mnorm, wa, wb, wo = lw
    g_ffn, wpqt, bd, ub, vtb = fw
    x2d = x.reshape(n, D_MODEL)
    tm = _row_tile(n, 256)
    tables = _rope_tables(pos)
    if tables_rows != t:
        tables = tuple(jnp.tile(a, (tables_rows // t, 1)) for a in tables)
    (mq, mk, mv, mo, aq, ak, av, akb, avb, iq, sg, sm, ik, ik2) = _inproj(x2d, g_mix, wp, bias_row, tables, tm)
    hn, cc, nn, mm = _mlstm(mq, mk, mv, sm, mo, g_mnorm, c0, n0, m0, nb, t)
    att = attend(aq, iq, sm, ak, av, akb, avb, ik, ik2)
    h1, xn2t, st = _merge(x2d, hn, att, sg, wa, wb, wo, g_ffn, wpqt, bd, tm)
    tn = _row_tile(n, 512)
    s1m, s2m, e1, e2, tau = _peer_prep(st, _row_tile(n, 256))
    y = _peer_dense(ub, vtb, xn2t, s1m, s2m, e1, e2, tau, h1, g_final, tn, 512)
    return (y.reshape(nb, t, D_MODEL), ak.reshape(nb, t, H_A, DH_A), av.reshape(nb, t, H_A, DH_A),
            ik.reshape(nb, t, D_IDX), cc, nn, mm)


def kernel(x_prompt, x_sample, cache_k, cache_v, cache_kidx, page_table, state_C, state_n, state_m,
           g_mix, w_in, b_mgate, g_mnorm, w_a, w_b, w_o, g_ffn, w_pq, sub_keys, peer_u, peer_v, g_final):
    depth = w_in.shape[0]
    assert depth == 1, "single-layer step"
    l = 0
    bp, sp, _ = x_prompt.shape
    bd_, td, _ = x_sample.shape
    past = page_table.shape[1] * PAGE_SIZE

    one = lambda a: a.reshape(a.shape[1:])
    bias_row = jnp.zeros((1, LANES), F32).at[0, SM_IP:SM_IW].set(b_mgate[l].astype(F32))
    lw = (g_mix[l], _prep_w_in(one(w_in)), bias_row, g_mnorm[l],
          one(w_a).astype(BF16), one(w_b).astype(BF16), one(w_o).astype(BF16))
    fw = (g_ffn[l], one(w_pq).T.astype(BF16), _prep_subkeys(one(sub_keys)),
          one(peer_u).astype(BF16), one(peer_v).T.astype(BF16))

    def attend_p(aq, iq, sm, ak, av, akb, avb, ik, ik2):
        return _dsa_prompt(aq, iq, sm, akb, avb, ik2, bp, sp)

    def attend_s(aq, iq, sm, ak, av, akb, avb, ik, ik2):
        return _dsa_sample(aq, iq, sm, ak, av, ik, one(cache_k), one(cache_v), one(cache_kidx), page_table, bd_, td)

    zc = jnp.zeros((bp, H_M, DV_M, DK_M), F32)
    zn = jnp.zeros((bp, H_M, DK_M), F32)
    zm = jnp.zeros((bp, H_M), F32)
    yp, kp, vp, kip, cp, np_, mp = _group(x_prompt, jnp.arange(sp), sp, attend_p, zc, zn, zm, lw, fw, g_final)
    ys, ks, vs, kis, cs, ns, ms = _group(x_sample, past + jnp.arange(td), bd_ * td, attend_s,
                                         one(state_C), one(state_n), one(state_m), lw, fw, g_final)
    st = lambda a: a[None]
    return (yp, ys, st(kp), st(vp), st(kip), st(cp), st(np_), st(mp),
            st(ks), st(vs), st(kis), st(cs), st(ns), st(ms))
```

```python
import functools
import math

import jax
import jax.numpy as jnp
from jax import lax
from jax.experimental import pallas as pl
from jax.experimental.pallas import tpu as pltpu

F32 = jnp.float32
BF16 = jnp.bfloat16
I32 = jnp.int32
I16 = jnp.int16

D_MODEL = 1024
PAGE_SIZE = 128
H_M, DK_M, DV_M, M_CHUNK = 4, 128, 128, 64
H_A, DH_A = 4, 128
H_I, D_IDX = 8, 64
TOPK_MAX = 256
Q_BLOCK = 128
ROPE_THETA = 10000.0
N_KEYS = 128
H_P, D_PKEY, P_TOPK = 8, 128, 16
EPS = 1e-6

MIX_SPLITS = (H_M * DK_M, H_M * DK_M, H_M * DV_M, H_M, H_M, H_M * DV_M,
              H_A * DH_A, H_A * DH_A, H_A * DH_A, H_I * D_IDX, D_IDX, H_I, 2 * D_MODEL)

LANES = 128
VMEM_LIMIT_BYTES = 56 * 1024 * 1024
NEG_INF_KEY = -2139095041
INT_MIN = -2147483648
NEG_BIG = -1e30

SM_IK, SM_IP, SM_LF, SM_IW, SM_END = 0, 64, 68, 72, 80
W_MQ, W_MK, W_MV, W_MO, W_AQ, W_AK, W_AV, W_IQ, W_G, W_SM, W_COLS = (
    0, 512, 1024, 1536, 2048, 2560, 3072, 3584, 4096, 6144, 6272)


def _dot(a, b):
    return jnp.dot(a, b, preferred_element_type=F32)


def _dot_nt(a, b):
    return lax.dot_general(a, b, (((1,), (1,)), ((), ())), preferred_element_type=F32)


def _dot_tn(a, b):
    return lax.dot_general(a, b, (((0,), (0,)), ((), ())), preferred_element_type=F32)


def _okey(x):
    i = lax.bitcast_convert_type(x, I32)
    return i ^ ((i >> 31) & jnp.int32(0x7FFFFFFF))


def _cparams(sem):
    return pltpu.CompilerParams(dimension_semantics=sem, vmem_limit_bytes=VMEM_LIMIT_BYTES)


def _transpose_cast_kernel(x_ref, o_ref):
    o_ref[...] = x_ref[...].T.astype(BF16)


def _transpose_cast(x, tr):
    r, c = x.shape
    assert r % tr == 0
    return pl.pallas_call(
        _transpose_cast_kernel,
        grid=(r // tr,),
        in_specs=[pl.BlockSpec((tr, c), lambda i: (i, 0))],
        out_specs=pl.BlockSpec((c, tr), lambda i: (0, i)),
        out_shape=jax.ShapeDtypeStruct((c, r), BF16),
        compiler_params=_cparams(("parallel",)),
        name="transpose_cast",
    )(x)


def _inproj_kernel(x_ref, g_ref, w_ref, bias_ref, cosa_ref, sina_ref, cosi_ref, sini_ref,
                   mq_ref, mk_ref, mv_ref, mo_ref, aq_ref, ak_ref, av_ref, akb_ref, avb_ref,
                   iq_ref, sg_ref, sm_ref, ik_ref, ik2_ref):
    x = x_ref[...]
    ms = jnp.mean(x * x, axis=-1, keepdims=True)
    xn = ((x * lax.rsqrt(ms + EPS)) * g_ref[...]).astype(BF16)

    def proj(lo, width):
        return _dot(xn, w_ref[:, lo:lo + width])

    cosa, sina = cosa_ref[...], sina_ref[...]
    cosi, sini = cosi_ref[...], sini_ref[...]
    lane = lax.broadcasted_iota(I32, (x.shape[0], LANES), 1)
    first_half = (lane % 64) < 32

    def rope128(z):
        outs = []
        for h in range(z.shape[1] // LANES):
            zh = z[:, h * LANES:(h + 1) * LANES]
            outs.append(zh * cosa + pltpu.roll(zh, 64, 1) * sina)
        return jnp.concatenate(outs, axis=1) if len(outs) > 1 else outs[0]

    def rope64_slab(zh):
        rot = jnp.where(first_half, pltpu.roll(zh, 96, 1), pltpu.roll(zh, 32, 1))
        return zh * cosi + rot * sini

    mq_ref[...] = (proj(W_MQ, 512) * (DK_M ** -0.5)).astype(BF16)
    mk_ref[...] = proj(W_MK, 512).astype(BF16)
    mv_ref[...] = proj(W_MV, 512).astype(BF16)
    mo_ref[...] = proj(W_MO, 512)
    aq_ref[...] = rope128(proj(W_AQ, 512)).astype(BF16)
    ak = rope128(proj(W_AK, 512))
    ak_ref[...] = ak
    akb_ref[...] = ak.astype(BF16)
    av = proj(W_AV, 512)
    av_ref[...] = av
    avb_ref[...] = av.astype(BF16)
    ziq = proj(W_IQ, 512)
    iq = jnp.concatenate([rope64_slab(ziq[:, h * LANES:(h + 1) * LANES]) for h in range(4)], axis=1)
    iq_ref[...] = (iq * (D_IDX ** -0.5)).astype(BF16)
    sg_ref[...] = jax.nn.sigmoid(proj(W_G, 2 * D_MODEL))
    zs = proj(W_SM, LANES)
    ikr = rope64_slab(zs)
    zb = zs + bias_ref[...]
    logsig = jnp.minimum(zb, 0.0) - jnp.log1p(jnp.exp(-jnp.abs(zb)))
    sm = jnp.where(lane < SM_IP, ikr,
                   jnp.where(lane < SM_LF, zb,
                             jnp.where(lane < SM_IW, logsig,
                                       jnp.where(lane < SM_END, zs * (H_I ** -0.5), 0.0))))
    sm_ref[...] = sm
    ik_ref[...] = ikr[:, :D_IDX]
    ik2_ref[...] = jnp.where(lane < D_IDX, ikr, pltpu.roll(ikr, 64, 1)).astype(BF16)


def _prep_w_in(w_in):
    parts, o = [], 0
    for s in MIX_SPLITS:
        parts.append(w_in[:, o:o + s])
        o += s
    mq, mk, mv, mi, mf, mo, aq, ak, av, iq, ik, iw, gates = parts
    pad = jnp.zeros((w_in.shape[0], LANES - SM_END), w_in.dtype)
    small = jnp.concatenate([ik, mi, mf, iw, pad], axis=1)
    return jnp.concatenate([mq, mk, mv, mo, aq, ak, av, iq, gates, small], axis=1).astype(BF16)


def _rope_tables(pos):
    pos = pos.astype(F32)[:, None]

    def tab(d):
        inv = ROPE_THETA ** (-jnp.arange(0, d, 2, dtype=F32) / d)
        ang = pos * inv[None, :]
        c, s = jnp.cos(ang), jnp.sin(ang)
        cos = jnp.concatenate([c, c], axis=1)
        sin = jnp.concatenate([-s, s], axis=1)
        reps = LANES // d
        return jnp.tile(cos, (1, reps)), jnp.tile(sin, (1, reps))

    cosa, sina = tab(DH_A)
    cosi, sini = tab(D_IDX)
    return cosa, sina, cosi, sini


def _inproj(x2d, g_mix, wp, bias_row, tables, tm):
    n = x2d.shape[0]
    t_tab = tables[0].shape[0]
    assert n % tm == 0 and t_tab % tm == 0
    nt = t_tab // tm
    row = lambda i: (i, 0)
    tabspec = pl.BlockSpec((tm, LANES), lambda i: (i % nt, 0))
    full = lambda shape: pl.BlockSpec(shape, lambda i: (0, 0))
    out_defs = [
        (512, BF16), (512, BF16), (512, BF16), (512, F32), (512, BF16), (512, F32), (512, F32),
        (512, BF16), (512, BF16), (512, BF16), (2 * D_MODEL, F32), (LANES, F32), (D_IDX, F32), (LANES, BF16)]
    return pl.pallas_call(
        _inproj_kernel,
        grid=(n // tm,),
        in_specs=[pl.BlockSpec((tm, D_MODEL), row), full((1, D_MODEL)), full((D_MODEL, W_COLS)),
                  full((1, LANES)), tabspec, tabspec, tabspec, tabspec],
        out_specs=[pl.BlockSpec((tm, w), row) for w, _ in out_defs],
        out_shape=[jax.ShapeDtypeStruct((n, w), dt) for w, dt in out_defs],
        compiler_params=_cparams(("parallel",)),
        name="inproj",
    )(x2d, g_mix.reshape(1, D_MODEL), wp, bias_row, *tables)


def _mlstm_kernel(q_ref, k_ref, v_ref, sm_ref, mo_ref, gm_ref, c0_ref, n0_ref, m0_ref,
                  hn_ref, c_ref, n_ref, m_ref, *, c, nchunks):
    @pl.when(pl.program_id(1) == 0)
    def _():
        c_ref[...] = c0_ref[...]
        n_ref[...] = n0_ref[...]
        m_ref[...] = m0_ref[...]

    ri = lax.broadcasted_iota(I32, (c, c), 0)
    ci = lax.broadcasted_iota(I32, (c, c), 1)
    eye = ri == ci
    tri = ri >= ci

    def to_row(col):
        return jnp.sum(jnp.where(eye, jnp.broadcast_to(col, (c, c)), 0.0), axis=0, keepdims=True)

    def chunk(rows):
        q = q_ref[0, rows, :]
        k = k_ref[0, rows, :]
        v = v_ref[0, rows, :]
        sm = sm_ref[0, rows, :]
        mo = mo_ref[0, rows, :]
        hs = range(H_M)
        sls = [slice(h * LANES, (h + 1) * LANES) for h in hs]
        state = [(m_ref[0, h][:, :1], n_ref[0, h], c_ref[0, h]) for h in hs]
        qk = [_dot_nt(q[:, sl], k[:, sl]) for sl in sls]
        qc = [_dot_nt(q[:, sls[h]], state[h][2].astype(BF16)) for h in hs]
        gate = []
        for h in hs:
            ipc = sm[:, SM_IP + h:SM_IP + h + 1]
            lfc = sm[:, SM_LF + h:SM_LF + h + 1]
            lf_row = to_row(lfc)
            ip_row = to_row(ipc)
            b_col = jnp.sum(jnp.where(tri, jnp.broadcast_to(lf_row, (c, c)), 0.0), axis=1, keepdims=True)
            b_row = to_row(b_col)
            m_prev = state[h][0]
            m_inter = b_col + m_prev
            dmat = jnp.where(tri, b_col - b_row + ip_row, -jnp.inf)
            m_t = jnp.maximum(m_inter, jnp.max(dmat, axis=1, keepdims=True))
            m_new = m_t[c - 1:c, :]
            b_last = b_col[c - 1:c, :]
            gate.append(dict(m_t=m_t, w_inter=jnp.exp(m_inter - m_t), decay=jnp.exp(dmat - m_t), m_new=m_new,
                             g_state=jnp.exp(b_last + m_prev - m_new),
                             g_tok=jnp.exp(b_last - b_col + ipc - m_new)))
        s_mat = [qk[h] * gate[h]["decay"] for h in hs]
        sv = [_dot(s_mat[h].astype(BF16), v[:, sls[h]]) for h in hs]
        upd = [_dot_tn((gate[h]["g_tok"] * v[:, sls[h]].astype(F32)).astype(BF16), k[:, sls[h]]) for h in hs]
        for h in hs:
            g, sl = gate[h], sls[h]
            _, n_prev, c_prev = state[h]
            num = g["w_inter"] * qc[h] + sv[h]
            den = (g["w_inter"] * jnp.sum(q[:, sl].astype(F32) * n_prev, axis=1, keepdims=True)
                   + jnp.sum(s_mat[h], axis=1, keepdims=True))
            hh = num / jnp.maximum(jnp.abs(den), jnp.exp(-g["m_t"]))
            y = hh * lax.rsqrt(jnp.mean(hh * hh, axis=1, keepdims=True) + EPS) * gm_ref[:, sl]
            hn_ref[0, rows, sl] = (jax.nn.sigmoid(mo[:, sl]) * y).astype(BF16)
            c_ref[0, h] = g["g_state"] * c_prev + upd[h]
            n_ref[0, h] = (g["g_state"] * n_prev
                           + jnp.sum(g["g_tok"] * k[:, sl].astype(F32), axis=0, keepdims=True))
            m_ref[0, h] = jnp.broadcast_to(g["m_new"], (1, LANES))

    if nchunks == 1:
        chunk(slice(0, c))
    else:
        def body(i, carry):
            chunk(pl.ds(pl.multiple_of(i * c, c), c))
            return carry
        lax.fori_loop(0, nchunks, body, 0)


def _mlstm(mq, mk, mv, sm, mo, g_mnorm, c0, n0, m0, nb, t):
    c = math.gcd(M_CHUNK, t)
    tb = min(t, 8 * c)
    assert t % tb == 0
    r3 = lambda a: a.reshape(nb, t, a.shape[-1])
    n0b = n0.reshape(nb, H_M, 1, DK_M).astype(F32)
    m0b = jnp.broadcast_to(m0.astype(F32)[:, :, None, None], (nb, H_M, 1, LANES))
    tok = lambda w: pl.BlockSpec((1, tb, w), lambda b, j: (b, j, 0))
    st4 = lambda shape: pl.BlockSpec(shape, lambda b, j: (b, 0, 0, 0))
    hn, cc, nn, mm = pl.pallas_call(
        functools.partial(_mlstm_kernel, c=c, nchunks=tb // c),
        grid=(nb, t // tb),
        in_specs=[tok(512), tok(512), tok(512), tok(LANES), tok(512),
                  pl.BlockSpec((1, H_M * DV_M), lambda b, j: (0, 0)),
                  st4((1, H_M, DV_M, DK_M)), st4((1, H_M, 1, DK_M)), st4((1, H_M, 1, LANES))],
        out_specs=[tok(512), st4((1, H_M, DV_M, DK_M)), st4((1, H_M, 1, DK_M)), st4((1, H_M, 1, LANES))],
        out_shape=[jax.ShapeDtypeStruct((nb, t, H_M * DV_M), BF16),
                   jax.ShapeDtypeStruct((nb, H_M, DV_M, DK_M), F32),
                   jax.ShapeDtypeStruct((nb, H_M, 1, DK_M), F32),
                   jax.ShapeDtypeStruct((nb, H_M, 1, LANES), F32)],
        compiler_params=_cparams(("parallel", "arbitrary")),
        name="mlstm",
    )(r3(mq), r3(mk), r3(mv), r3(sm), r3(mo), g_mnorm.reshape(1, H_M * DV_M),
      c0.astype(F32), n0b, m0b)
    return hn.reshape(nb * t, H_M * DV_M), cc, nn.reshape(nb, H_M, DK_M), mm[:, :, 0, 0]


def _kth_largest_key(count_ge, k, shape):
    kf = jnp.float32(k)
    r0 = jnp.where(count_ge(jnp.zeros(shape, I32)) >= kf, jnp.int32(0), jnp.int32(INT_MIN))

    def body(t, r):
        cand = r + lax.shift_left(jnp.int32(1), jnp.int32(30) - t)
        return jnp.where(count_ge(cand) >= kf, cand, r)

    return lax.fori_loop(0, 31, body, r0)


def _dsa_prompt_kernel(aq_ref, iq_ref, sm_ref, kb_ref, vb_ref, ik2_ref, out_ref, skey_ref, acc_ref, mx_ref, l_ref,
                       hi_ref, lo_ref, *, kc, topk):
    i = pl.program_id(1)
    qb = Q_BLOCK
    nk = (i * qb + qb + kc - 1) // kc
    iq = iq_ref[0]
    lane = lax.broadcasted_iota(I32, (qb, LANES), 1)
    zero = jnp.zeros((qb, LANES), BF16)
    lhs = []
    for hp in range(H_I // 2):
        slab = iq[:, hp * LANES:(hp + 1) * LANES]
        lhs.append(jnp.where(lane < D_IDX, slab, zero))
        lhs.append(jnp.where(lane >= D_IDX, slab, zero))
    sm = sm_ref[0]
    qpos = i * qb + lax.broadcasted_iota(I32, (qb, 1), 0)

    def score_body(cidx, carry):
        k0 = pl.multiple_of(cidx * kc, kc)
        kk = ik2_ref[0, pl.ds(k0, kc), :]
        prods = [_dot_nt(lhs[h], kk) for h in range(H_I)]
        sc = jnp.zeros((qb, kc), F32)
        for h in range(H_I):
            sc = sc + sm[:, SM_IW + h:SM_IW + h + 1] * jnp.maximum(prods[h], 0.0)
        kpos = k0 + lax.broadcasted_iota(I32, (1, kc), 1)
        sc = jnp.where(kpos <= qpos, sc, -jnp.inf)
        key = _okey(sc)
        skey_ref[:, pl.ds(k0, kc)] = key
        hi_ref[:, pl.ds(k0, kc)] = (key >> 16).astype(I16)
        lo_ref[:, pl.ds(k0, kc)] = ((key & 0xFFFF) - 32768).astype(I16)
        return carry

    lax.fori_loop(0, nk, score_body, 0)

    def count_ge16(ref, cand):
        c16 = cand.astype(I16)
        one, zero = jnp.ones((), I16), jnp.zeros((), I16)

        def body(cidx, acc):
            k0 = pl.multiple_of(cidx * kc, kc)
            m = jnp.where(ref[:, pl.ds(k0, kc)] >= c16, one, zero)
            part = m[:, :LANES]
            for t in range(1, kc // LANES):
                part = part + m[:, t * LANES:(t + 1) * LANES]
            return acc + part
        acc = lax.fori_loop(0, nk, body, jnp.zeros((qb, LANES), I16))
        return jnp.sum(acc.astype(F32), axis=1, keepdims=True)

    def kth16(ref, kf):
        r0 = jnp.where(count_ge16(ref, jnp.zeros((qb, 1), I32)) >= kf, jnp.int32(0), jnp.int32(-32768))

        def body(t, r):
            cand = r + lax.shift_left(jnp.int32(1), jnp.int32(14) - t)
            return jnp.where(count_ge16(ref, cand) >= kf, cand, r)

        return lax.fori_loop(0, 15, body, r0)

    kf = jnp.full((qb, 1), float(topk), F32)
    p_hi = kth16(hi_ref, kf)
    above = jnp.where(p_hi < 32767, count_ge16(hi_ref, jnp.minimum(p_hi + 1, 32767)), 0.0)
    p16 = p_hi.astype(I16)

    def bucket_body(cidx, carry):
        k0 = pl.multiple_of(cidx * kc, kc)
        inb = hi_ref[:, pl.ds(k0, kc)] == p16
        lo_ref[:, pl.ds(k0, kc)] = jnp.where(inb, lo_ref[:, pl.ds(k0, kc)], jnp.full((), -32768, I16))
        return carry

    lax.fori_loop(0, nk, bucket_body, 0)
    p_lo = kth16(lo_ref, kf - above)
    thr = lax.shift_left(p_hi, jnp.int32(16)) | (p_lo + 32768)
    thr = jnp.maximum(thr, jnp.int32(NEG_INF_KEY + 1))
    scale = DH_A ** -0.5
    aq = aq_ref[0]
    ntile = kc // LANES
    heads = [slice(h * LANES, (h + 1) * LANES) for h in range(H_A)]

    def fold_tiles(x, op):
        r = x[:, :LANES]
        for u in range(1, ntile):
            r = op(r, x[:, u * LANES:(u + 1) * LANES])
        return r

    mx_ref[...] = jnp.full(mx_ref.shape, NEG_BIG, F32)
    l_ref[...] = jnp.zeros(l_ref.shape, F32)
    acc_ref[...] = jnp.zeros(acc_ref.shape, F32)

    def bias_body(cidx, carry):
        k0 = pl.multiple_of(cidx * kc, kc)
        bias = jnp.where(skey_ref[:, pl.ds(k0, kc)] >= thr, 0.0, NEG_BIG)
        skey_ref[:, pl.ds(k0, kc)] = lax.bitcast_convert_type(bias, I32)
        return carry

    lax.fori_loop(0, nk, bias_body, 0)
    load_bias = lambda k0: lax.bitcast_convert_type(skey_ref[:, pl.ds(k0, kc)], F32)

    def max_body(cidx, carry):
        k0 = pl.multiple_of(cidx * kc, kc)
        bias = load_bias(k0)
        lgs = [_dot_nt(aq[:, sl], kb_ref[0, pl.ds(k0, kc), sl]) for sl in heads]
        for h in range(H_A):
            mx_ref[h] = jnp.maximum(mx_ref[h], fold_tiles(lgs[h] + bias, jnp.maximum))
        return carry

    lax.fori_loop(0, nk, max_body, 0)
    for h in range(H_A):
        mx_ref[h] = jnp.broadcast_to(jnp.max(mx_ref[h], axis=1, keepdims=True) * scale, (qb, LANES))

    def sum_body(cidx, carry):
        k0 = pl.multiple_of(cidx * kc, kc)
        bias = load_bias(k0)
        lgs = [_dot_nt(aq[:, sl], kb_ref[0, pl.ds(k0, kc), sl]) for sl in heads]
        ps = []
        for h in range(H_A):
            lg = lgs[h] * scale + bias
            p = jnp.exp(lg - jnp.concatenate([mx_ref[h]] * ntile, axis=1))
            l_ref[h] += fold_tiles(p, jnp.add)
            ps.append(p.astype(BF16))
        for h, sl in enumerate(heads):
            acc_ref[h] += _dot(ps[h], vb_ref[0, pl.ds(k0, kc), sl])
        return carry

    lax.fori_loop(0, nk, sum_body, 0)
    for h, sl in enumerate(heads):
        out_ref[0, :, sl] = (acc_ref[h] / jnp.sum(l_ref[h], axis=1, keepdims=True)).astype(BF16)


def _dsa_prompt(aq, iq, sm, akb, avb, ik2, nb, s):
    topk = min(TOPK_MAX, s // 4)
    kc = min(512, s)
    assert s % kc == 0 and s % Q_BLOCK == 0 and kc % Q_BLOCK == 0
    r3 = lambda a: a.reshape(nb, s, a.shape[-1])
    qspec = lambda w: pl.BlockSpec((1, Q_BLOCK, w), lambda b, i: (b, i, 0))
    kspec = lambda w: pl.BlockSpec((1, s, w), lambda b, i: (b, 0, 0))
    out = pl.pallas_call(
        functools.partial(_dsa_prompt_kernel, kc=kc, topk=topk),
        grid=(nb, s // Q_BLOCK),
        in_specs=[qspec(512), qspec(512), qspec(LANES), kspec(512), kspec(512), kspec(LANES)],
        out_specs=qspec(512),
        out_shape=jax.ShapeDtypeStruct((nb, s, H_A * DH_A), BF16),
        scratch_shapes=[pltpu.VMEM((Q_BLOCK, s), I32)] + [pltpu.VMEM((H_A, Q_BLOCK, LANES), F32)] * 3
                       + [pltpu.VMEM((Q_BLOCK, s), I16)] * 2,
        compiler_params=_cparams(("parallel", "arbitrary")),
        name="dsa_prompt",
    )(r3(aq), r3(iq), r3(sm), r3(akb), r3(avb), r3(ik2))
    return out.reshape(nb * s, H_A * DH_A)


def _idx_lhs(iq_ref):
    iqf = iq_ref[0].astype(F32)
    return jnp.concatenate([iqf[:, h * D_IDX:(h + 1) * D_IDX] for h in range(H_I)], axis=0).astype(BF16)


def _idx_score(r, sm, t):
    sc = jnp.zeros((t, r.shape[1]), F32)
    for h in range(H_I):
        sc = sc + sm[:, SM_IW + h:SM_IW + h + 1] * jnp.maximum(r[h * t:(h + 1) * t, :], 0.0)
    return sc


def _dsa_s_score_kernel(pt_ref, iq_ref, sm_ref, ikn_ref, *rest, pgs, t, topk, npast):
    pages = rest[:pgs]
    skey_ref, skn_ref, thr_ref = rest[pgs:]
    j = pl.program_id(1)
    lhs = _idx_lhs(iq_ref)
    sm = sm_ref[0]
    kall = jnp.concatenate([pages[r][0] for r in range(pgs)], axis=0).astype(BF16)
    base = pl.multiple_of(j * (pgs * PAGE_SIZE), pgs * PAGE_SIZE)
    skey_ref[0, :, pl.ds(base, pgs * PAGE_SIZE)] = _okey(_idx_score(_dot_nt(lhs, kall), sm, t))

    @pl.when(j == pl.num_programs(1) - 1)
    def _():
        scn = _idx_score(_dot_nt(lhs, ikn_ref[0]), sm, t)
        ti = lax.broadcasted_iota(I32, (t, LANES), 0)
        si = lax.broadcasted_iota(I32, (t, LANES), 1)
        kn = _okey(jnp.where(si <= ti, scn, -jnp.inf))
        skn_ref[0] = kn
        cw = 1024 if npast % 1024 == 0 else PAGE_SIZE

        def count_ge(cand):
            def body(cidx, acc):
                k0 = pl.multiple_of(cidx * cw, cw)
                m = jnp.where(skey_ref[0, :, pl.ds(k0, cw)] >= cand, 1.0, 0.0)
                part = m[:, :LANES]
                for u in range(1, cw // LANES):
                    part = part + m[:, u * LANES:(u + 1) * LANES]
                return acc + part
            acc = lax.fori_loop(0, npast // cw, body, jnp.where(kn >= cand, 1.0, 0.0))
            return jnp.sum(acc, axis=1, keepdims=True)

        thr = _kth_largest_key(count_ge, topk, (t, 1))
        thr = jnp.maximum(thr, jnp.int32(NEG_INF_KEY + 1))
        thr_ref[0] = jnp.broadcast_to(thr, (t, LANES))


def _dsa_s_att_kernel(pt_ref, aq_ref, skey_ref, skn_ref, thr_ref, kn_ref, vn_ref, emat_ref, hbias_ref, *rest,
                      pg, t):
    kpages, vpages = rest[:pg], rest[pg:2 * pg]
    out_ref, m_sc, l_sc, acc_sc = rest[2 * pg:]
    j = pl.program_id(1)

    @pl.when(j == 0)
    def _():
        m_sc[...] = jnp.full(m_sc.shape, NEG_BIG, F32)
        l_sc[...] = jnp.zeros(l_sc.shape, F32)
        acc_sc[...] = jnp.zeros(acc_sc.shape, F32)

    scale = DH_A ** -0.5
    aqf = aq_ref[0].astype(F32)
    qs = jnp.concatenate([aqf[:, h * LANES:(h + 1) * LANES] for h in range(H_A)], axis=0).astype(BF16)
    thr = thr_ref[0][:, :1]
    rows = lambda a, h: a[h * t:(h + 1) * t]

    def update(lg, bias, pv_of):
        lg = lg * scale + bias
        m_i, l_i = m_sc[:, :1], l_sc[:, :1]
        m_new = jnp.maximum(m_i, jnp.max(lg, axis=1, keepdims=True))
        alpha = jnp.exp(m_i - m_new)
        p = jnp.exp(lg - m_new)
        l_new = alpha * l_i + jnp.sum(p, axis=1, keepdims=True)
        acc_sc[...] = alpha * acc_sc[...] + pv_of(p.astype(BF16))
        m_sc[...] = jnp.broadcast_to(m_new, m_sc.shape)
        l_sc[...] = jnp.broadcast_to(l_new, l_sc.shape)

    ncol = pg * PAGE_SIZE * H_A
    kall = jnp.concatenate([kpages[r][0] for r in range(pg)], axis=0).astype(BF16)
    vall = jnp.concatenate([vpages[r][0] for r in range(pg)], axis=0).astype(BF16)
    picked = jnp.where(skey_ref[0] >= thr, 1.0, 0.0)
    picked = jnp.concatenate([picked] * H_A, axis=0).astype(BF16)
    spread = jnp.concatenate([_dot(picked[:, r * PAGE_SIZE:(r + 1) * PAGE_SIZE], emat_ref[...])
                              for r in range(pg)], axis=1)
    update(_dot_nt(qs, kall), hbias_ref[...] + (spread - 1.0) * (-NEG_BIG), lambda pb: _dot(pb, vall))

    @pl.when(j == pl.num_programs(1) - 1)
    def _():
        bias1 = jnp.where(skn_ref[0] >= thr, 0.0, NEG_BIG)
        hs = lambda ref, h: ref[0, :, h * LANES:(h + 1) * LANES]
        lg = jnp.concatenate([rows(_dot_nt(qs, hs(kn_ref, h)), h) for h in range(H_A)], axis=0)
        update(lg, jnp.concatenate([bias1] * H_A, axis=0),
               lambda pb: jnp.concatenate([rows(_dot(pb, hs(vn_ref, h)), h) for h in range(H_A)], axis=0))
        o = acc_sc[...] / l_sc[:, :1]
        for h in range(H_A):
            out_ref[0, :, h * LANES:(h + 1) * LANES] = rows(o, h).astype(BF16)


def _dsa_sample(aq, iq, sm, ak, av, ik, cache_k, cache_v, cache_kidx, page_table, nb, t):
    n_pages = page_table.shape[1]
    npast = n_pages * PAGE_SIZE
    topk = min(TOPK_MAX, (npast + t) // 4)
    pg = math.gcd(8, n_pages)
    pgs = math.gcd(32, n_pages)
    assert t & (t - 1) == 0 and H_A & (H_A - 1) == 0
    n_pool = cache_k.shape[0]
    ck = cache_k.reshape(n_pool, PAGE_SIZE * H_A, DH_A)
    cv = cache_v.reshape(n_pool, PAGE_SIZE * H_A, DH_A)
    emat = (jnp.arange(PAGE_SIZE * H_A)[None, :] // H_A == jnp.arange(PAGE_SIZE)[:, None]).astype(BF16)
    same_head = (jnp.arange(pg * PAGE_SIZE * H_A)[None, :] % H_A) == (jnp.arange(H_A * t)[:, None] // t)
    hbias = jnp.where(same_head, 0.0, NEG_BIG).astype(F32)
    r3 = lambda a: a.reshape(nb, t, a.shape[-1])
    padrows = lambda a: jnp.pad(r3(a), ((0, 0), (0, LANES - t), (0, 0))).astype(BF16)
    tokspec = lambda w: pl.BlockSpec((1, t, w), lambda b, j, pt: (b, 0, 0))
    newspec = lambda w: pl.BlockSpec((1, LANES, w), lambda b, j, pt: (b, 0, 0))

    def pagespec(rows_, w, r, per_step):
        return pl.BlockSpec((1, rows_, w), lambda b, j, pt, r=r: (pt[b, j * per_step + r], 0, 0))

    skey, skn, thr = pl.pallas_call(
        functools.partial(_dsa_s_score_kernel, pgs=pgs, t=t, topk=topk, npast=npast),
        grid_spec=pltpu.PrefetchScalarGridSpec(
            num_scalar_prefetch=1,
            grid=(nb, n_pages // pgs),
            in_specs=[tokspec(512), tokspec(LANES), newspec(D_IDX)]
                     + [pagespec(PAGE_SIZE, D_IDX, r, pgs) for r in range(pgs)],
            out_specs=[tokspec(npast), tokspec(LANES), tokspec(LANES)],
        ),
        out_shape=[jax.ShapeDtypeStruct((nb, t, npast), I32),
                   jax.ShapeDtypeStruct((nb, t, LANES), I32),
                   jax.ShapeDtypeStruct((nb, t, LANES), I32)],
        compiler_params=_cparams(("parallel", "arbitrary")),
        name="dsa_sample_score",
    )(page_table, r3(iq), r3(sm), padrows(ik), *([cache_kidx] * pgs))

    att = pl.pallas_call(
        functools.partial(_dsa_s_att_kernel, pg=pg, t=t),
        grid_spec=pltpu.PrefetchScalarGridSpec(
            num_scalar_prefetch=1,
            grid=(nb, n_pages // pg),
            in_specs=[tokspec(512),
                      pl.BlockSpec((1, t, pg * PAGE_SIZE), lambda b, j, pt: (b, 0, j)),
                      tokspec(LANES), tokspec(LANES), newspec(512), newspec(512),
                      pl.BlockSpec(emat.shape, lambda b, j, pt: (0, 0)),
                      pl.BlockSpec(hbias.shape, lambda b, j, pt: (0, 0))]
                     + [pagespec(PAGE_SIZE * H_A, DH_A, r, pg) for r in range(pg)] * 2,
            out_specs=tokspec(512),
            scratch_shapes=[pltpu.VMEM((H_A * t, LANES), F32)] * 3,
        ),
        out_shape=jax.ShapeDtypeStruct((nb, t, H_A * DH_A), BF16),
        compiler_params=_cparams(("parallel", "arbitrary")),
        name="dsa_sample_att",
    )(page_table, r3(aq), skey, skn, thr, padrows(ak), padrows(av), emat, hbias, *([ck] * pg), *([cv] * pg))
    return att.reshape(nb * t, H_A * DH_A)


def _merge_kernel(x_ref, hn_ref, att_ref, sg_ref, wa_ref, wb_ref, wo_ref, gf_ref, wpqt_ref, bd_ref,
                  h1_ref, xn2t_ref, st_ref):
    sg = sg_ref[...]
    merged = sg[:, :D_MODEL] * _dot(hn_ref[...], wa_ref[...]) + sg[:, D_MODEL:] * _dot(att_ref[...], wb_ref[...])
    h1 = x_ref[...] + _dot(merged.astype(BF16), wo_ref[...])
    h1_ref[...] = h1
    xn2 = (h1 * lax.rsqrt(jnp.mean(h1 * h1, axis=-1, keepdims=True) + EPS)) * gf_ref[...]
    xn2t = xn2.T.astype(BF16)
    xn2t_ref[...] = xn2t
    qt = _dot(wpqt_ref[...], xn2t).astype(BF16)
    for h in range(H_P):
        st_ref[h * 2 * N_KEYS:(h + 1) * 2 * N_KEYS, :] = _dot(bd_ref[h], qt[h * D_PKEY:(h + 1) * D_PKEY, :])


def _prep_subkeys(sub_keys):
    z = jnp.zeros_like(sub_keys[:, 0])
    top = jnp.concatenate([sub_keys[:, 0], z], axis=-1)
    bot = jnp.concatenate([z, sub_keys[:, 1]], axis=-1)
    return jnp.concatenate([top, bot], axis=1).astype(BF16)


def _merge(x2d, hn, att, sg, wa, wb, wo, g_ffn, wpqt, bd, tm):
    n = x2d.shape[0]
    assert n % tm == 0
    row = lambda w: pl.BlockSpec((tm, w), lambda i: (i, 0))
    full2 = lambda a: pl.BlockSpec(a.shape, lambda i: (0, 0))
    return pl.pallas_call(
        _merge_kernel,
        grid=(n // tm,),
        in_specs=[row(D_MODEL), row(512), row(512), row(2 * D_MODEL), full2(wa), full2(wb), full2(wo),
                  pl.BlockSpec((1, D_MODEL), lambda i: (0, 0)), full2(wpqt),
                  pl.BlockSpec(bd.shape, lambda i: (0, 0, 0))],
        out_specs=[row(D_MODEL), pl.BlockSpec((D_MODEL, tm), lambda i: (0, i)),
                   pl.BlockSpec((H_P * 2 * N_KEYS, tm), lambda i: (0, i))],
        out_shape=[jax.ShapeDtypeStruct((n, D_MODEL), F32), jax.ShapeDtypeStruct((D_MODEL, n), BF16),
                   jax.ShapeDtypeStruct((H_P * 2 * N_KEYS, n), F32)],
        compiler_params=_cparams(("parallel",)),
        name="merge",
    )(x2d, hn, att, sg, wa, wb, wo, g_ffn.reshape(1, D_MODEL), wpqt, bd)


def _extract_top(s, k, first_only=True):
    rows = lax.broadcasted_iota(I32, s.shape, 0)
    cur, vals = s, []
    for _ in range(k):
        m = jnp.max(cur, axis=0, keepdims=True)
        if first_only:
            first = jnp.min(jnp.where(cur == m, rows, s.shape[0]), axis=0, keepdims=True)
            cur = jnp.where(rows == first, -jnp.inf, cur)
        else:
            cur = jnp.where(cur == m, -jnp.inf, cur)
        vals.append(m)
    return vals, cur


_STAIR = [(a, b) for a in range(P_TOPK) for b in range(P_TOPK) if (a + 1) * (b + 1) <= P_TOPK]


def _peer_prep_kernel(st_ref, s1m_ref, s2m_ref, e1_ref, e2_ref, tau_ref, top_ref, sel_ref):
    tn = st_ref.shape[1]
    taus = []

    def top_keys(rows, slot):
        def run(first_only):
            s = st_ref[rows, :]
            vals, res = _extract_top(s, P_TOPK, first_only)
            picked = jnp.where(res != s, 1.0, 0.0)
            top_ref[slot] = jnp.concatenate(vals, axis=0)
            sel_ref[slot] = picked
            return picked
        picked = run(False)
        npick = jnp.sum(picked, axis=0, keepdims=True)
        bad = jnp.sum(jnp.where(npick == float(P_TOPK), 0.0, 1.0))

        @pl.when(bad > 0.0)
        def _():
            run(True)

    for h in range(H_P):
        r1s = slice((2 * h) * N_KEYS, (2 * h + 1) * N_KEYS)
        r2s = slice((2 * h + 1) * N_KEYS, (2 * h + 2) * N_KEYS)
        top_keys(r1s, 0)
        top_keys(r2s, 1)
        s1, s2 = st_ref[r1s, :], st_ref[r2s, :]
        sel1, sel2 = sel_ref[0] > 0.0, sel_ref[1] > 0.0
        v1 = [top_ref[0, a:a + 1, :] for a in range(P_TOPK)]
        v2 = [top_ref[1, a:a + 1, :] for a in range(P_TOPK)]
        cand = [v1[a] + v2[b] for a, b in _STAIR]
        npad = (-len(cand)) % 8
        cand = jnp.concatenate(cand + [jnp.full((npad, tn), -jnp.inf, F32)], axis=0)
        top, _ = _extract_top(cand, P_TOPK)
        z = jnp.zeros_like(top[0])
        for c in top:
            z = z + jnp.exp(c - top[0])
        taus.append(top[P_TOPK - 1])
        sl = slice(h * N_KEYS, (h + 1) * N_KEYS)
        outs = ((s1m_ref, jnp.where(sel1, s1, -jnp.inf)), (s2m_ref, jnp.where(sel2, s2, -jnp.inf)),
                (e1_ref, jnp.where(sel1, jnp.exp(s1 - v1[0]), 0.0) / z),
                (e2_ref, jnp.where(sel2, jnp.exp(s2 - v2[0]), 0.0)))
        for ref, val in outs:
            for lt in range(tn // LANES):
                ref[lt, sl, :] = val[:, lt * LANES:(lt + 1) * LANES]
    tau = jnp.concatenate(taus, axis=0)
    for lt in range(tn // LANES):
        tau_ref[lt] = tau[:, lt * LANES:(lt + 1) * LANES]


def _peer_prep(st, tn):
    n = st.shape[1]
    assert n % tn == 0 and tn % LANES == 0
    hk = H_P * N_KEYS
    col = lambda r: pl.BlockSpec((tn // LANES, r, LANES), lambda i: (i, 0, 0))
    return pl.pallas_call(
        _peer_prep_kernel,
        grid=(n // tn,),
        in_specs=[pl.BlockSpec((2 * hk, tn), lambda i: (0, i))],
        out_specs=[col(hk), col(hk), col(hk), col(hk), col(H_P)],
        out_shape=[jax.ShapeDtypeStruct((n // LANES, hk, LANES), F32)] * 4
                  + [jax.ShapeDtypeStruct((n // LANES, H_P, LANES), F32)],
        scratch_shapes=[pltpu.VMEM((2, P_TOPK, tn), F32), pltpu.VMEM((2, N_KEYS, tn), F32)],
        compiler_params=_cparams(("parallel",)),
        name="peer_prep",
    )(st)


def _peer_dense_kernel(u_ref, vt_ref, xn2t_ref, s1m_ref, s2m_ref, e1_ref, e2_ref, tau_ref, h1_ref, gf_ref,
                       y_ref, acc_ref, a_ref, yb_ref):
    e = pl.program_id(1)
    te, tn = u_ref.shape[0], xn2t_ref.shape[1]
    nblk, nlt = te // N_KEYS, tn // LANES

    @pl.when(e == 0)
    def _():
        acc_ref[...] = jnp.zeros(acc_ref.shape, F32)

    a = _dot(u_ref[...], xn2t_ref[...])
    for lt in range(nlt):
        a_ref[lt] = a[:, lt * LANES:(lt + 1) * LANES]

    def token_tile(lt, carry):
        def expert_tile(blk, carry2):
            b0 = pl.multiple_of(blk * N_KEYS, N_KEYS)
            a_idx = e * nblk + blk
            w = jnp.zeros((N_KEYS, LANES), F32)
            for h in range(H_P):
                sl = slice(h * N_KEYS, (h + 1) * N_KEYS)
                s1row = s1m_ref[lt, pl.ds(h * N_KEYS + a_idx, 1), :]
                e1row = e1_ref[lt, pl.ds(h * N_KEYS + a_idx, 1), :]
                t = s2m_ref[lt, sl, :] + s1row
                w = w + jnp.where(t >= tau_ref[lt, h:h + 1, :], e2_ref[lt, sl, :], 0.0) * e1row
            a_t = a_ref[lt, pl.ds(b0, N_KEYS), :]
            gl = 0.5 * a_t * (1.0 + lax.erf(a_t * (2.0 ** -0.5)))
            yb_ref[lt, pl.ds(b0, N_KEYS), :] = (w * gl).astype(BF16)
            return carry2

        lax.fori_loop(0, nblk, expert_tile, 0)
        return carry

    lax.fori_loop(0, nlt, token_tile, 0)
    yb = jnp.concatenate([yb_ref[lt] for lt in range(nlt)], axis=1)
    acc_ref[...] += _dot(vt_ref[...], yb)

    @pl.when(e == pl.num_programs(1) - 1)
    def _():
        h2 = h1_ref[...] + acc_ref[...].T
        y_ref[...] = (h2 * lax.rsqrt(jnp.mean(h2 * h2, axis=-1, keepdims=True) + EPS)) * gf_ref[...]


def _peer_dense(ub, vtb, xn2t, s1m, s2m, e1, e2, tau, h1, g_final, tn, te):
    n = xn2t.shape[1]
    ne = ub.shape[0]
    assert n % tn == 0 and ne % te == 0 and tn % LANES == 0 and te % N_KEYS == 0
    hk = H_P * N_KEYS
    col = lambda r: pl.BlockSpec((tn // LANES, r, LANES), lambda i, e: (i, 0, 0))
    row = lambda w: pl.BlockSpec((tn, w), lambda i, e: (i, 0))
    return pl.pallas_call(
        _peer_dense_kernel,
        grid=(n // tn, ne // te),
        in_specs=[pl.BlockSpec((te, D_MODEL), lambda i, e: (e, 0)),
                  pl.BlockSpec((D_MODEL, te), lambda i, e: (0, e)),
                  pl.BlockSpec((D_MODEL, tn), lambda i, e: (0, i)),
                  col(hk), col(hk), col(hk), col(hk), col(H_P), row(D_MODEL),
                  pl.BlockSpec((1, D_MODEL), lambda i, e: (0, 0))],
        out_specs=row(D_MODEL),
        out_shape=jax.ShapeDtypeStruct((n, D_MODEL), F32),
        scratch_shapes=[pltpu.VMEM((D_MODEL, tn), F32), pltpu.VMEM((tn // LANES, te, LANES), F32),
                        pltpu.VMEM((tn // LANES, te, LANES), BF16)],
        compiler_params=_cparams(("parallel", "arbitrary")),
        name="peer_dense",
    )(ub, vtb, xn2t, s1m, s2m, e1, e2, tau, h1, g_final.reshape(1, D_MODEL))


def _row_tile(n, pref):
    t = pref
    while n % t:
        t //= 2
    return t


def _group(x, pos, tables_rows, attend, c0, n0, m0, lw, fw, g_final):
    nb, t, _ = x.shape
    n = nb * t
    g_mix, wp, bias_row, g_mnorm, wa, wb, wo = lw
    g_ffn, wpqt, bd, ub, vtb = fw
    x2d = x.reshape(n, D_MODEL)
    tm = _row_tile(n, 256)
    tables = _rope_tables(pos)
    if tables_rows != t:
        tables = tuple(jnp.tile(a, (tables_rows // t, 1)) for a in tables)
    (mq, mk, mv, mo, aq, ak, av, akb, avb, iq, sg, sm, ik, ik2) = _inproj(x2d, g_mix, wp, bias_row, tables, tm)
    hn, cc, nn, mm = _mlstm(mq, mk, mv, sm, mo, g_mnorm, c0, n0, m0, nb, t)
    att = attend(aq, iq, sm, ak, av, akb, avb, ik, ik2)
    h1, xn2t, st = _merge(x2d, hn, att, sg, wa, wb, wo, g_ffn, wpqt, bd, tm)
    tn = _row_tile(n, 512)
    s1m, s2m, e1, e2, tau = _peer_prep(st, _row_tile(n, 256))
    y = _peer_dense(ub, vtb, xn2t, s1m, s2m, e1, e2, tau, h1, g_final, tn, 512)
    return (y.reshape(nb, t, D_MODEL), ak.reshape(nb, t, H_A, DH_A), av.reshape(nb, t, H_A, DH_A),
            ik.reshape(nb, t, D_IDX), cc, nn, mm)


def kernel(x_prompt, x_sample, cache_k, cache_v, cache_kidx, page_table, state_C, state_n, state_m,
           g_mix, w_in, b_mgate, g_mnorm, w_a, w_b, w_o, g_ffn, w_pq, sub_keys, peer_u, peer_v, g_final):
    depth = w_in.shape[0]
    assert depth == 1, "single-layer step"
    l = 0
    bp, sp, _ = x_prompt.shape
    bd_, td, _ = x_sample.shape
    past = page_table.shape[1] * PAGE_SIZE

    one = lambda a: a.reshape(a.shape[1:])
    bias_row = jnp.zeros((1, LANES), F32).at[0, SM_IP:SM_IW].set(b_mgate[l].astype(F32))
    lw = (g_mix[l], _prep_w_in(one(w_in)), bias_row, g_mnorm[l],
          one(w_a).astype(BF16), one(w_b).astype(BF16), one(w_o).astype(BF16))
    fw = (g_ffn[l], one(w_pq).T.astype(BF16), _prep_subkeys(one(sub_keys)),
          one(peer_u).astype(BF16), _transpose_cast(one(peer_v), 512))

    def attend_p(aq, iq, sm, ak, av, akb, avb, ik, ik2):
        return _dsa_prompt(aq, iq, sm, akb, avb, ik2, bp, sp)

    def attend_s(aq, iq, sm, ak, av, akb, avb, ik, ik2):
        return _dsa_sample(aq, iq, sm, ak, av, ik, one(cache_k), one(cache_v), one(cache_kidx), page_table, bd_, td)

    zc = jnp.zeros((bp, H_M, DV_M, DK_M), F32)
    zn = jnp.zeros((bp, H_M, DK_M), F32)
    zm = jnp.zeros((bp, H_M), F32)
    yp, kp, vp, kip, cp, np_, mp = _group(x_prompt, jnp.arange(sp), sp, attend_p, zc, zn, zm, lw, fw, g_final)
    ys, ks, vs, kis, cs, ns, ms = _group(x_sample, past + jnp.arange(td), bd_ * td, attend_s,
                                         one(state_C), one(state_n), one(state_m), lw, fw, g_final)
    st = lambda a: a[None]
    return (yp, ys, st(kp), st(vp), st(kip), st(cp), st(np_), st(mp),
            st(ks), st(vs), st(kis), st(cs), st(ns), st(ms))
```

```python
import functools
import math

import jax
import jax.numpy as jnp
from jax import lax
from jax.experimental import pallas as pl
from jax.experimental.pallas import tpu as pltpu

F32 = jnp.float32
BF16 = jnp.bfloat16
I32 = jnp.int32
I16 = jnp.int16

D_MODEL = 1024
PAGE_SIZE = 128
H_M, DK_M, DV_M, M_CHUNK = 4, 128, 128, 64
H_A, DH_A = 4, 128
H_I, D_IDX = 8, 64
TOPK_MAX = 256
Q_BLOCK = 128
ROPE_THETA = 10000.0
N_KEYS = 128
H_P, D_PKEY, P_TOPK = 8, 128, 16
EPS = 1e-6

MIX_SPLITS = (H_M * DK_M, H_M * DK_M, H_M * DV_M, H_M, H_M, H_M * DV_M,
              H_A * DH_A, H_A * DH_A, H_A * DH_A, H_I * D_IDX, D_IDX, H_I, 2 * D_MODEL)

LANES = 128
VMEM_LIMIT_BYTES = 56 * 1024 * 1024
NEG_INF_KEY = -2139095041
INT_MIN = -2147483648
NEG_BIG = -1e30

SM_IK, SM_IP, SM_LF, SM_IW, SM_END = 0, 64, 68, 72, 80
W_MQ, W_MK, W_MV, W_MO, W_AQ, W_AK, W_AV, W_IQ, W_G, W_SM, W_COLS = (
    0, 512, 1024, 1536, 2048, 2560, 3072, 3584, 4096, 6144, 6272)


def _dot(a, b):
    return jnp.dot(a, b, preferred_element_type=F32)


def _dot_nt(a, b):
    return lax.dot_general(a, b, (((1,), (1,)), ((), ())), preferred_element_type=F32)


def _dot_tn(a, b):
    return lax.dot_general(a, b, (((0,), (0,)), ((), ())), preferred_element_type=F32)


def _okey(x):
    i = lax.bitcast_convert_type(x, I32)
    return i ^ ((i >> 31) & jnp.int32(0x7FFFFFFF))


def _cparams(sem):
    return pltpu.CompilerParams(dimension_semantics=sem, vmem_limit_bytes=VMEM_LIMIT_BYTES)


def _transpose_cast_kernel(x_ref, o_ref):
    o_ref[...] = x_ref[...].T.astype(BF16)


def _transpose_cast(x, tr):
    r, c = x.shape
    assert r % tr == 0
    return pl.pallas_call(
        _transpose_cast_kernel,
        grid=(r // tr,),
        in_specs=[pl.BlockSpec((tr, c), lambda i: (i, 0))],
        out_specs=pl.BlockSpec((c, tr), lambda i: (0, i)),
        out_shape=jax.ShapeDtypeStruct((c, r), BF16),
        compiler_params=_cparams(("parallel",)),
        name="transpose_cast",
    )(x)


def _inproj_kernel(x_ref, g_ref, w_ref, bias_ref, cosa_ref, sina_ref, cosi_ref, sini_ref,
                   mq_ref, mk_ref, mv_ref, mo_ref, aq_ref, ak_ref, av_ref, akb_ref, avb_ref,
                   iq_ref, sg_ref, sm_ref, ik_ref, ik2_ref, aqt_ref, avt_ref, iqt_ref, smt_ref):
    x = x_ref[...]
    ms = jnp.mean(x * x, axis=-1, keepdims=True)
    xn = ((x * lax.rsqrt(ms + EPS)) * g_ref[...]).astype(BF16)

    def proj(lo, width):
        return _dot(xn, w_ref[:, lo:lo + width])

    cosa, sina = cosa_ref[...], sina_ref[...]
    cosi, sini = cosi_ref[...], sini_ref[...]
    lane = lax.broadcasted_iota(I32, (x.shape[0], LANES), 1)
    first_half = (lane % 64) < 32

    def rope128(z):
        outs = []
        for h in range(z.shape[1] // LANES):
            zh = z[:, h * LANES:(h + 1) * LANES]
            outs.append(zh * cosa + pltpu.roll(zh, 64, 1) * sina)
        return jnp.concatenate(outs, axis=1) if len(outs) > 1 else outs[0]

    def rope64_slab(zh):
        rot = jnp.where(first_half, pltpu.roll(zh, 96, 1), pltpu.roll(zh, 32, 1))
        return zh * cosi + rot * sini

    mq_ref[...] = (proj(W_MQ, 512) * (DK_M ** -0.5)).astype(BF16)
    mk_ref[...] = proj(W_MK, 512).astype(BF16)
    mv_ref[...] = proj(W_MV, 512).astype(BF16)
    mo_ref[...] = proj(W_MO, 512)
    aq = rope128(proj(W_AQ, 512))
    aq_ref[...] = aq.astype(BF16)
    aqt_ref[...] = aq.T.astype(BF16)
    ak = rope128(proj(W_AK, 512))
    ak_ref[...] = ak
    akb_ref[...] = ak.astype(BF16)
    av = proj(W_AV, 512)
    av_ref[...] = av
    avb_ref[...] = av.astype(BF16)
    avt_ref[...] = av.T.astype(BF16)
    ziq = proj(W_IQ, 512)
    iq = jnp.concatenate([rope64_slab(ziq[:, h * LANES:(h + 1) * LANES]) for h in range(4)], axis=1)
    iq = iq * (D_IDX ** -0.5)
    iq_ref[...] = iq.astype(BF16)
    iqt_ref[...] = iq.T.astype(BF16)
    sg_ref[...] = jax.nn.sigmoid(proj(W_G, 2 * D_MODEL))
    zs = proj(W_SM, LANES)
    ikr = rope64_slab(zs)
    zb = zs + bias_ref[...]
    logsig = jnp.minimum(zb, 0.0) - jnp.log1p(jnp.exp(-jnp.abs(zb)))
    sm = jnp.where(lane < SM_IP, ikr,
                   jnp.where(lane < SM_LF, zb,
                             jnp.where(lane < SM_IW, logsig,
                                       jnp.where(lane < SM_END, zs * (H_I ** -0.5), 0.0))))
    sm_ref[...] = sm
    smt_ref[...] = sm.T
    ik_ref[...] = ikr[:, :D_IDX]
    ik2_ref[...] = jnp.where(lane < D_IDX, ikr, pltpu.roll(ikr, 64, 1)).astype(BF16)


def _prep_w_in(w_in):
    parts, o = [], 0
    for s in MIX_SPLITS:
        parts.append(w_in[:, o:o + s])
        o += s
    mq, mk, mv, mi, mf, mo, aq, ak, av, iq, ik, iw, gates = parts
    pad = jnp.zeros((w_in.shape[0], LANES - SM_END), w_in.dtype)
    small = jnp.concatenate([ik, mi, mf, iw, pad], axis=1)
    return jnp.concatenate([mq, mk, mv, mo, aq, ak, av, iq, gates, small], axis=1).astype(BF16)


def _rope_tables(pos):
    pos = pos.astype(F32)[:, None]

    def tab(d):
        inv = ROPE_THETA ** (-jnp.arange(0, d, 2, dtype=F32) / d)
        ang = pos * inv[None, :]
        c, s = jnp.cos(ang), jnp.sin(ang)
        cos = jnp.concatenate([c, c], axis=1)
        sin = jnp.concatenate([-s, s], axis=1)
        reps = LANES // d
        return jnp.tile(cos, (1, reps)), jnp.tile(sin, (1, reps))

    cosa, sina = tab(DH_A)
    cosi, sini = tab(D_IDX)
    return cosa, sina, cosi, sini


def _inproj(x2d, g_mix, wp, bias_row, tables, tm):
    n = x2d.shape[0]
    t_tab = tables[0].shape[0]
    assert n % tm == 0 and t_tab % tm == 0
    nt = t_tab // tm
    row = lambda i: (i, 0)
    tabspec = pl.BlockSpec((tm, LANES), lambda i: (i % nt, 0))
    full = lambda shape: pl.BlockSpec(shape, lambda i: (0, 0))
    out_defs = [
        (512, BF16), (512, BF16), (512, BF16), (512, F32), (512, BF16), (512, F32), (512, F32),
        (512, BF16), (512, BF16), (512, BF16), (2 * D_MODEL, F32), (LANES, F32), (D_IDX, F32), (LANES, BF16)]
    t_defs = [(512, BF16), (512, BF16), (512, BF16), (LANES, F32)]
    return pl.pallas_call(
        _inproj_kernel,
        grid=(n // tm,),
        in_specs=[pl.BlockSpec((tm, D_MODEL), row), full((1, D_MODEL)), full((D_MODEL, W_COLS)),
                  full((1, LANES)), tabspec, tabspec, tabspec, tabspec],
        out_specs=[pl.BlockSpec((tm, w), row) for w, _ in out_defs]
                  + [pl.BlockSpec((w, tm), lambda i: (0, i)) for w, _ in t_defs],
        out_shape=[jax.ShapeDtypeStruct((n, w), dt) for w, dt in out_defs]
                  + [jax.ShapeDtypeStruct((w, n), dt) for w, dt in t_defs],
        compiler_params=_cparams(("parallel",)),
        name="inproj",
    )(x2d, g_mix.reshape(1, D_MODEL), wp, bias_row, *tables)


def _mlstm_kernel(q_ref, k_ref, v_ref, sm_ref, mo_ref, gm_ref, c0_ref, n0_ref, m0_ref,
                  hn_ref, c_ref, n_ref, m_ref, *, c, nchunks):
    @pl.when(pl.program_id(1) == 0)
    def _():
        c_ref[...] = c0_ref[...]
        n_ref[...] = n0_ref[...]
        m_ref[...] = m0_ref[...]

    ri = lax.broadcasted_iota(I32, (c, c), 0)
    ci = lax.broadcasted_iota(I32, (c, c), 1)
    eye = ri == ci
    tri = ri >= ci

    def to_row(col):
        return jnp.sum(jnp.where(eye, jnp.broadcast_to(col, (c, c)), 0.0), axis=0, keepdims=True)

    def chunk(rows):
        q = q_ref[0, rows, :]
        k = k_ref[0, rows, :]
        v = v_ref[0, rows, :]
        sm = sm_ref[0, rows, :]
        mo = mo_ref[0, rows, :]
        hs = range(H_M)
        sls = [slice(h * LANES, (h + 1) * LANES) for h in hs]
        state = [(m_ref[0, h][:, :1], n_ref[0, h], c_ref[0, h]) for h in hs]
        qk = [_dot_nt(q[:, sl], k[:, sl]) for sl in sls]
        qc = [_dot_nt(q[:, sls[h]], state[h][2].astype(BF16)) for h in hs]
        gate = []
        for h in hs:
            ipc = sm[:, SM_IP + h:SM_IP + h + 1]
            lfc = sm[:, SM_LF + h:SM_LF + h + 1]
            lf_row = to_row(lfc)
            ip_row = to_row(ipc)
            b_col = jnp.sum(jnp.where(tri, jnp.broadcast_to(lf_row, (c, c)), 0.0), axis=1, keepdims=True)
            b_row = to_row(b_col)
            m_prev = state[h][0]
            m_inter = b_col + m_prev
            dmat = jnp.where(tri, b_col - b_row + ip_row, -jnp.inf)
            m_t = jnp.maximum(m_inter, jnp.max(dmat, axis=1, keepdims=True))
            m_new = m_t[c - 1:c, :]
            b_last = b_col[c - 1:c, :]
            gate.append(dict(m_t=m_t, w_inter=jnp.exp(m_inter - m_t), decay=jnp.exp(dmat - m_t), m_new=m_new,
                             g_state=jnp.exp(b_last + m_prev - m_new),
                             g_tok=jnp.exp(b_last - b_col + ipc - m_new)))
        s_mat = [qk[h] * gate[h]["decay"] for h in hs]
        sv = [_dot(s_mat[h].astype(BF16), v[:, sls[h]]) for h in hs]
        upd = [_dot_tn((gate[h]["g_tok"] * v[:, sls[h]].astype(F32)).astype(BF16), k[:, sls[h]]) for h in hs]
        for h in hs:
            g, sl = gate[h], sls[h]
            _, n_prev, c_prev = state[h]
            num = g["w_inter"] * qc[h] + sv[h]
            den = (g["w_inter"] * jnp.sum(q[:, sl].astype(F32) * n_prev, axis=1, keepdims=True)
                   + jnp.sum(s_mat[h], axis=1, keepdims=True))
            hh = num / jnp.maximum(jnp.abs(den), jnp.exp(-g["m_t"]))
            y = hh * lax.rsqrt(jnp.mean(hh * hh, axis=1, keepdims=True) + EPS) * gm_ref[:, sl]
            hn_ref[0, rows, sl] = (jax.nn.sigmoid(mo[:, sl]) * y).astype(BF16)
            c_ref[0, h] = g["g_state"] * c_prev + upd[h]
            n_ref[0, h] = (g["g_state"] * n_prev
                           + jnp.sum(g["g_tok"] * k[:, sl].astype(F32), axis=0, keepdims=True))
            m_ref[0, h] = jnp.broadcast_to(g["m_new"], (1, LANES))

    if nchunks == 1:
        chunk(slice(0, c))
    else:
        def body(i, carry):
            chunk(pl.ds(pl.multiple_of(i * c, c), c))
            return carry
        lax.fori_loop(0, nchunks, body, 0)


def _mlstm(mq, mk, mv, sm, mo, g_mnorm, c0, n0, m0, nb, t):
    c = math.gcd(M_CHUNK, t)
    tb = min(t, 8 * c)
    assert t % tb == 0
    r3 = lambda a: a.reshape(nb, t, a.shape[-1])
    n0b = n0.reshape(nb, H_M, 1, DK_M).astype(F32)
    m0b = jnp.broadcast_to(m0.astype(F32)[:, :, None, None], (nb, H_M, 1, LANES))
    tok = lambda w: pl.BlockSpec((1, tb, w), lambda b, j: (b, j, 0))
    st4 = lambda shape: pl.BlockSpec(shape, lambda b, j: (b, 0, 0, 0))
    hn, cc, nn, mm = pl.pallas_call(
        functools.partial(_mlstm_kernel, c=c, nchunks=tb // c),
        grid=(nb, t // tb),
        in_specs=[tok(512), tok(512), tok(512), tok(LANES), tok(512),
                  pl.BlockSpec((1, H_M * DV_M), lambda b, j: (0, 0)),
                  st4((1, H_M, DV_M, DK_M)), st4((1, H_M, 1, DK_M)), st4((1, H_M, 1, LANES))],
        out_specs=[tok(512), st4((1, H_M, DV_M, DK_M)), st4((1, H_M, 1, DK_M)), st4((1, H_M, 1, LANES))],
        out_shape=[jax.ShapeDtypeStruct((nb, t, H_M * DV_M), BF16),
                   jax.ShapeDtypeStruct((nb, H_M, DV_M, DK_M), F32),
                   jax.ShapeDtypeStruct((nb, H_M, 1, DK_M), F32),
                   jax.ShapeDtypeStruct((nb, H_M, 1, LANES), F32)],
        compiler_params=_cparams(("parallel", "arbitrary")),
        name="mlstm",
    )(r3(mq), r3(mk), r3(mv), r3(sm), r3(mo), g_mnorm.reshape(1, H_M * DV_M),
      c0.astype(F32), n0b, m0b)
    return hn.reshape(nb * t, H_M * DV_M), cc, nn.reshape(nb, H_M, DK_M), mm[:, :, 0, 0]


def _kth_largest_key(count_ge, k, shape):
    kf = jnp.float32(k)
    r0 = jnp.where(count_ge(jnp.zeros(shape, I32)) >= kf, jnp.int32(0), jnp.int32(INT_MIN))

    def body(t, r):
        cand = r + lax.shift_left(jnp.int32(1), jnp.int32(30) - t)
        return jnp.where(count_ge(cand) >= kf, cand, r)

    return lax.fori_loop(0, 31, body, r0)


def _dsa_prompt_kernel(aq_ref, iq_ref, sm_ref, kb_ref, vb_ref, ik2_ref, out_ref, skey_ref, acc_ref, mx_ref, l_ref,
                       hi_ref, lo_ref, *, kc, topk):
    i = pl.program_id(1)
    qb = Q_BLOCK
    nk = (i * qb + qb + kc - 1) // kc
    iq = iq_ref[0]
    lane = lax.broadcasted_iota(I32, (qb, LANES), 1)
    zero = jnp.zeros((qb, LANES), BF16)
    lhs = []
    for hp in range(H_I // 2):
        slab = iq[:, hp * LANES:(hp + 1) * LANES]
        lhs.append(jnp.where(lane < D_IDX, slab, zero))
        lhs.append(jnp.where(lane >= D_IDX, slab, zero))
    sm = sm_ref[0]
    qpos = i * qb + lax.broadcasted_iota(I32, (qb, 1), 0)

    def score_body(cidx, carry):
        k0 = pl.multiple_of(cidx * kc, kc)
        kk = ik2_ref[0, pl.ds(k0, kc), :]
        prods = [_dot_nt(lhs[h], kk) for h in range(H_I)]
        sc = jnp.zeros((qb, kc), F32)
        for h in range(H_I):
            sc = sc + sm[:, SM_IW + h:SM_IW + h + 1] * jnp.maximum(prods[h], 0.0)
        kpos = k0 + lax.broadcasted_iota(I32, (1, kc), 1)
        sc = jnp.where(kpos <= qpos, sc, -jnp.inf)
        key = _okey(sc)
        skey_ref[:, pl.ds(k0, kc)] = key
        hi_ref[:, pl.ds(k0, kc)] = (key >> 16).astype(I16)
        lo_ref[:, pl.ds(k0, kc)] = ((key & 0xFFFF) - 32768).astype(I16)
        return carry

    lax.fori_loop(0, nk, score_body, 0)

    def count_ge16(ref, cand):
        c16 = cand.astype(I16)
        one, zero = jnp.ones((), I16), jnp.zeros((), I16)

        def body(cidx, acc):
            k0 = pl.multiple_of(cidx * kc, kc)
            m = jnp.where(ref[:, pl.ds(k0, kc)] >= c16, one, zero)
            part = m[:, :LANES]
            for t in range(1, kc // LANES):
                part = part + m[:, t * LANES:(t + 1) * LANES]
            return acc + part
        acc = lax.fori_loop(0, nk, body, jnp.zeros((qb, LANES), I16))
        return jnp.sum(acc.astype(F32), axis=1, keepdims=True)

    def kth16(ref, kf):
        r0 = jnp.where(count_ge16(ref, jnp.zeros((qb, 1), I32)) >= kf, jnp.int32(0), jnp.int32(-32768))

        def body(t, r):
            cand = r + lax.shift_left(jnp.int32(1), jnp.int32(14) - t)
            return jnp.where(count_ge16(ref, cand) >= kf, cand, r)

        return lax.fori_loop(0, 15, body, r0)

    kf = jnp.full((qb, 1), float(topk), F32)
    p_hi = kth16(hi_ref, kf)
    above = jnp.where(p_hi < 32767, count_ge16(hi_ref, jnp.minimum(p_hi + 1, 32767)), 0.0)
    p16 = p_hi.astype(I16)

    def bucket_body(cidx, carry):
        k0 = pl.multiple_of(cidx * kc, kc)
        inb = hi_ref[:, pl.ds(k0, kc)] == p16
        lo_ref[:, pl.ds(k0, kc)] = jnp.where(inb, lo_ref[:, pl.ds(k0, kc)], jnp.full((), -32768, I16))
        return carry

    lax.fori_loop(0, nk, bucket_body, 0)
    p_lo = kth16(lo_ref, kf - above)
    thr = lax.shift_left(p_hi, jnp.int32(16)) | (p_lo + 32768)
    thr = jnp.maximum(thr, jnp.int32(NEG_INF_KEY + 1))
    scale = DH_A ** -0.5
    aq = aq_ref[0]
    ntile = kc // LANES
    heads = [slice(h * LANES, (h + 1) * LANES) for h in range(H_A)]

    def fold_tiles(x, op):
        r = x[:, :LANES]
        for u in range(1, ntile):
            r = op(r, x[:, u * LANES:(u + 1) * LANES])
        return r

    mx_ref[...] = jnp.full(mx_ref.shape, NEG_BIG, F32)
    l_ref[...] = jnp.zeros(l_ref.shape, F32)
    acc_ref[...] = jnp.zeros(acc_ref.shape, F32)

    def bias_body(cidx, carry):
        k0 = pl.multiple_of(cidx * kc, kc)
        bias = jnp.where(skey_ref[:, pl.ds(k0, kc)] >= thr, 0.0, NEG_BIG)
        skey_ref[:, pl.ds(k0, kc)] = lax.bitcast_convert_type(bias, I32)
        return carry

    lax.fori_loop(0, nk, bias_body, 0)
    load_bias = lambda k0: lax.bitcast_convert_type(skey_ref[:, pl.ds(k0, kc)], F32)

    def max_body(cidx, carry):
        k0 = pl.multiple_of(cidx * kc, kc)
        bias = load_bias(k0)
        lgs = [_dot_nt(aq[:, sl], kb_ref[0, pl.ds(k0, kc), sl]) for sl in heads]
        for h in range(H_A):
            mx_ref[h] = jnp.maximum(mx_ref[h], fold_tiles(lgs[h] + bias, jnp.maximum))
        return carry

    lax.fori_loop(0, nk, max_body, 0)
    for h in range(H_A):
        mx_ref[h] = jnp.broadcast_to(jnp.max(mx_ref[h], axis=1, keepdims=True) * scale, (qb, LANES))

    def sum_body(cidx, carry):
        k0 = pl.multiple_of(cidx * kc, kc)
        bias = load_bias(k0)
        lgs = [_dot_nt(aq[:, sl], kb_ref[0, pl.ds(k0, kc), sl]) for sl in heads]
        ps = []
        for h in range(H_A):
            lg = lgs[h] * scale + bias
            p = jnp.exp(lg - jnp.concatenate([mx_ref[h]] * ntile, axis=1))
            l_ref[h] += fold_tiles(p, jnp.add)
            ps.append(p.astype(BF16))
        for h, sl in enumerate(heads):
            acc_ref[h] += _dot(ps[h], vb_ref[0, pl.ds(k0, kc), sl])
        return carry

    lax.fori_loop(0, nk, sum_body, 0)
    for h, sl in enumerate(heads):
        out_ref[0, :, sl] = (acc_ref[h] / jnp.sum(l_ref[h], axis=1, keepdims=True)).astype(BF16)


def _dsa_prompt(aq, iq, sm, akb, avb, ik2, nb, s):
    topk = min(TOPK_MAX, s // 4)
    kc = min(512, s)
    assert s % kc == 0 and s % Q_BLOCK == 0 and kc % Q_BLOCK == 0
    r3 = lambda a: a.reshape(nb, s, a.shape[-1])
    qspec = lambda w: pl.BlockSpec((1, Q_BLOCK, w), lambda b, i: (b, i, 0))
    kspec = lambda w: pl.BlockSpec((1, s, w), lambda b, i: (b, 0, 0))
    out = pl.pallas_call(
        functools.partial(_dsa_prompt_kernel, kc=kc, topk=topk),
        grid=(nb, s // Q_BLOCK),
        in_specs=[qspec(512), qspec(512), qspec(LANES), kspec(512), kspec(512), kspec(LANES)],
        out_specs=qspec(512),
        out_shape=jax.ShapeDtypeStruct((nb, s, H_A * DH_A), BF16),
        scratch_shapes=[pltpu.VMEM((Q_BLOCK, s), I32)] + [pltpu.VMEM((H_A, Q_BLOCK, LANES), F32)] * 3
                       + [pltpu.VMEM((Q_BLOCK, s), I16)] * 2,
        compiler_params=_cparams(("parallel", "arbitrary")),
        name="dsa_prompt",
    )(r3(aq), r3(iq), r3(sm), r3(akb), r3(avb), r3(ik2))
    return out.reshape(nb * s, H_A * DH_A)


def _dsa_prompt_t_kernel(aqt_ref, iqt_ref, smt_ref, kb_ref, vt_ref, ik2_ref, out_ref,
                         skey_ref, hi_ref, lo_ref, acc_ref, mx_ref, l_ref, *, kc, topk):
    i = pl.program_id(1)
    qb = Q_BLOCK
    nk = (i * qb + qb + kc - 1) // kc
    sub = 8
    iqt = iqt_ref[...]
    rowi = lax.broadcasted_iota(I32, (LANES, qb), 0)
    zero = jnp.zeros((LANES, qb), BF16)
    lhs = []
    for hp in range(H_I // 2):
        slab = iqt[hp * LANES:(hp + 1) * LANES, :]
        lhs.append(jnp.where(rowi < D_IDX, slab, zero))
        lhs.append(jnp.where(rowi >= D_IDX, slab, zero))
    wts = smt_ref[SM_IW:SM_IW + H_I, :]
    qpos = i * qb + lax.broadcasted_iota(I32, (1, qb), 1)

    def fold(x, op, rows):
        parts = [x[u * rows:(u + 1) * rows] for u in range(x.shape[0] // rows)]
        while len(parts) > 1:
            parts = [op(parts[u], parts[u + 1]) for u in range(0, len(parts) - 1, 2)] + parts[len(parts) & ~1:]
        return parts[0]

    def score_body(cidx, carry):
        k0 = pl.multiple_of(cidx * kc, kc)
        kk = ik2_ref[0, pl.ds(k0, kc), :]
        prods = [_dot(kk, lhs[h]) for h in range(H_I)]
        sc = jnp.zeros((kc, qb), F32)
        for h in range(H_I):
            sc = sc + wts[h:h + 1, :] * jnp.maximum(prods[h], 0.0)
        kpos = k0 + lax.broadcasted_iota(I32, (kc, 1), 0)
        key = _okey(jnp.where(kpos <= qpos, sc, -jnp.inf))
        skey_ref[pl.ds(k0, kc), :] = key
        hi_ref[pl.ds(k0, kc), :] = (key >> 16).astype(I16)
        lo_ref[pl.ds(k0, kc), :] = ((key & 0xFFFF) - 32768).astype(I16)
        return carry

    lax.fori_loop(0, nk, score_body, 0)

    def count_ge16(ref, cand):
        c16 = cand.astype(I16)
        one, zero16 = jnp.ones((), I16), jnp.zeros((), I16)

        def body(cidx, acc):
            k0 = pl.multiple_of(cidx * kc, kc)
            m = jnp.where(ref[pl.ds(k0, kc), :] >= c16, one, zero16)
            return acc + fold(m, jnp.add, 2 * sub)
        acc = lax.fori_loop(0, nk, body, jnp.zeros((2 * sub, qb), I16))
        return jnp.sum(acc.astype(F32), axis=0, keepdims=True)

    def kth16(ref, kf):
        r0 = jnp.where(count_ge16(ref, jnp.zeros((1, qb), I32)) >= kf, jnp.int32(0), jnp.int32(-32768))

        def body(t, r):
            cand = r + lax.shift_left(jnp.int32(1), jnp.int32(14) - t)
            return jnp.where(count_ge16(ref, cand) >= kf, cand, r)

        return lax.fori_loop(0, 15, body, r0)

    kf = jnp.full((1, qb), float(topk), F32)
    p_hi = kth16(hi_ref, kf)
    above = jnp.where(p_hi < 32767, count_ge16(hi_ref, jnp.minimum(p_hi + 1, 32767)), 0.0)
    p16 = p_hi.astype(I16)

    def bucket_body(cidx, carry):
        k0 = pl.multiple_of(cidx * kc, kc)
        inb = hi_ref[pl.ds(k0, kc), :] == p16
        lo_ref[pl.ds(k0, kc), :] = jnp.where(inb, lo_ref[pl.ds(k0, kc), :], jnp.full((), -32768, I16))
        return carry

    lax.fori_loop(0, nk, bucket_body, 0)
    p_lo = kth16(lo_ref, kf - above)
    thr = lax.shift_left(p_hi, jnp.int32(16)) | (p_lo + 32768)
    thr = jnp.maximum(thr, jnp.int32(NEG_INF_KEY + 1))

    def bias_body(cidx, carry):
        k0 = pl.multiple_of(cidx * kc, kc)
        bias = jnp.where(skey_ref[pl.ds(k0, kc), :] >= thr, 0.0, NEG_BIG)
        skey_ref[pl.ds(k0, kc), :] = lax.bitcast_convert_type(bias, I32)
        return carry

    lax.fori_loop(0, nk, bias_body, 0)
    load_bias = lambda k0: lax.bitcast_convert_type(skey_ref[pl.ds(k0, kc), :], F32)

    scale = DH_A ** -0.5
    aqt = aqt_ref[...]
    heads = [slice(h * LANES, (h + 1) * LANES) for h in range(H_A)]
    mx_ref[...] = jnp.full(mx_ref.shape, NEG_BIG, F32)
    l_ref[...] = jnp.zeros(l_ref.shape, F32)
    acc_ref[...] = jnp.zeros(acc_ref.shape, F32)

    def max_body(cidx, carry):
        k0 = pl.multiple_of(cidx * kc, kc)
        bias = load_bias(k0)
        lgs = [_dot(kb_ref[0, pl.ds(k0, kc), sl], aqt[sl, :]) for sl in heads]
        for h in range(H_A):
            mx_ref[h] = jnp.maximum(mx_ref[h], fold(lgs[h] + bias, jnp.maximum, sub))
        return carry

    lax.fori_loop(0, nk, max_body, 0)
    ms = [jnp.max(mx_ref[h], axis=0, keepdims=True) * scale for h in range(H_A)]

    def sum_body(cidx, carry):
        k0 = pl.multiple_of(cidx * kc, kc)
        bias = load_bias(k0)
        lgs = [_dot(kb_ref[0, pl.ds(k0, kc), sl], aqt[sl, :]) for sl in heads]
        ps = []
        for h in range(H_A):
            p = jnp.exp(lgs[h] * scale + bias - ms[h])
            l_ref[h] += fold(p, jnp.add, sub)
            ps.append(p.astype(BF16))
        for h, sl in enumerate(heads):
            acc_ref[h] += _dot(vt_ref[sl, pl.ds(k0, kc)], ps[h])
        return carry

    lax.fori_loop(0, nk, sum_body, 0)
    for h, sl in enumerate(heads):
        o = acc_ref[h] / jnp.sum(l_ref[h], axis=0, keepdims=True)
        out_ref[0, :, sl] = o.T.astype(BF16)


def _dsa_prompt_t(aqt, iqt, smt, akb, avt, ik2, nb, s):
    topk = min(TOPK_MAX, s // 4)
    kc = min(512, s)
    assert s % kc == 0 and s % Q_BLOCK == 0 and kc % Q_BLOCK == 0
    nblk = s // Q_BLOCK
    r3 = lambda a: a.reshape(nb, s, a.shape[-1])
    qcol = lambda w: pl.BlockSpec((w, Q_BLOCK), lambda b, i: (0, b * nblk + i))
    kspec = lambda w: pl.BlockSpec((1, s, w), lambda b, i: (b, 0, 0))
    out = pl.pallas_call(
        functools.partial(_dsa_prompt_t_kernel, kc=kc, topk=topk),
        grid=(nb, nblk),
        in_specs=[qcol(512), qcol(512), qcol(LANES), kspec(512),
                  pl.BlockSpec((H_A * DH_A, s), lambda b, i: (0, b)), kspec(LANES)],
        out_specs=pl.BlockSpec((1, Q_BLOCK, H_A * DH_A), lambda b, i: (b, i, 0)),
        out_shape=jax.ShapeDtypeStruct((nb, s, H_A * DH_A), BF16),
        scratch_shapes=[pltpu.VMEM((s, Q_BLOCK), I32), pltpu.VMEM((s, Q_BLOCK), I16), pltpu.VMEM((s, Q_BLOCK), I16),
                        pltpu.VMEM((H_A, DH_A, Q_BLOCK), F32), pltpu.VMEM((H_A, 8, Q_BLOCK), F32),
                        pltpu.VMEM((H_A, 8, Q_BLOCK), F32)],
        compiler_params=_cparams(("parallel", "arbitrary")),
        name="dsa_prompt",
    )(aqt, iqt, smt, r3(akb), avt, r3(ik2))
    return out.reshape(nb * s, H_A * DH_A)


def _idx_lhs(iq_ref):
    iqf = iq_ref[0].astype(F32)
    return jnp.concatenate([iqf[:, h * D_IDX:(h + 1) * D_IDX] for h in range(H_I)], axis=0).astype(BF16)


def _idx_score(r, sm, t):
    sc = jnp.zeros((t, r.shape[1]), F32)
    for h in range(H_I):
        sc = sc + sm[:, SM_IW + h:SM_IW + h + 1] * jnp.maximum(r[h * t:(h + 1) * t, :], 0.0)
    return sc


def _dsa_s_score_kernel(pt_ref, iq_ref, sm_ref, ikn_ref, *rest, pgs, t, topk, npast):
    pages = rest[:pgs]
    skey_ref, skn_ref, thr_ref = rest[pgs:]
    j = pl.program_id(1)
    lhs = _idx_lhs(iq_ref)
    sm = sm_ref[0]
    kall = jnp.concatenate([pages[r][0] for r in range(pgs)], axis=0).astype(BF16)
    base = pl.multiple_of(j * (pgs * PAGE_SIZE), pgs * PAGE_SIZE)
    skey_ref[0, :, pl.ds(base, pgs * PAGE_SIZE)] = _okey(_idx_score(_dot_nt(lhs, kall), sm, t))

    @pl.when(j == pl.num_programs(1) - 1)
    def _():
        scn = _idx_score(_dot_nt(lhs, ikn_ref[0]), sm, t)
        ti = lax.broadcasted_iota(I32, (t, LANES), 0)
        si = lax.broadcasted_iota(I32, (t, LANES), 1)
        kn = _okey(jnp.where(si <= ti, scn, -jnp.inf))
        skn_ref[0] = kn
        cw = 1024 if npast % 1024 == 0 else PAGE_SIZE

        def count_ge(cand):
            def body(cidx, acc):
                k0 = pl.multiple_of(cidx * cw, cw)
                m = jnp.where(skey_ref[0, :, pl.ds(k0, cw)] >= cand, 1.0, 0.0)
                part = m[:, :LANES]
                for u in range(1, cw // LANES):
                    part = part + m[:, u * LANES:(u + 1) * LANES]
                return acc + part
            acc = lax.fori_loop(0, npast // cw, body, jnp.where(kn >= cand, 1.0, 0.0))
            return jnp.sum(acc, axis=1, keepdims=True)

        thr = _kth_largest_key(count_ge, topk, (t, 1))
        thr = jnp.maximum(thr, jnp.int32(NEG_INF_KEY + 1))
        thr_ref[0] = jnp.broadcast_to(thr, (t, LANES))


def _dsa_s_att_kernel(pt_ref, aq_ref, skey_ref, skn_ref, thr_ref, kn_ref, vn_ref, emat_ref, hbias_ref, *rest,
                      pg, t):
    kpages, vpages = rest[:pg], rest[pg:2 * pg]
    out_ref, m_sc, l_sc, acc_sc = rest[2 * pg:]
    j = pl.program_id(1)

    @pl.when(j == 0)
    def _():
        m_sc[...] = jnp.full(m_sc.shape, NEG_BIG, F32)
        l_sc[...] = jnp.zeros(l_sc.shape, F32)
        acc_sc[...] = jnp.zeros(acc_sc.shape, F32)

    scale = DH_A ** -0.5
    aqf = aq_ref[0].astype(F32)
    qs = jnp.concatenate([aqf[:, h * LANES:(h + 1) * LANES] for h in range(H_A)], axis=0).astype(BF16)
    thr = thr_ref[0][:, :1]
    rows = lambda a, h: a[h * t:(h + 1) * t]

    def update(lg, bias, pv_of):
        lg = lg * scale + bias
        m_i, l_i = m_sc[:, :1], l_sc[:, :1]
        m_new = jnp.maximum(m_i, jnp.max(lg, axis=1, keepdims=True))
        alpha = jnp.exp(m_i - m_new)
        p = jnp.exp(lg - m_new)
        l_new = alpha * l_i + jnp.sum(p, axis=1, keepdims=True)
        acc_sc[...] = alpha * acc_sc[...] + pv_of(p.astype(BF16))
        m_sc[...] = jnp.broadcast_to(m_new, m_sc.shape)
        l_sc[...] = jnp.broadcast_to(l_new, l_sc.shape)

    ncol = pg * PAGE_SIZE * H_A
    kall = jnp.concatenate([kpages[r][0] for r in range(pg)], axis=0).astype(BF16)
    vall = jnp.concatenate([vpages[r][0] for r in range(pg)], axis=0).astype(BF16)
    picked = jnp.where(skey_ref[0] >= thr, 1.0, 0.0)
    picked = jnp.concatenate([picked] * H_A, axis=0).astype(BF16)
    spread = jnp.concatenate([_dot(picked[:, r * PAGE_SIZE:(r + 1) * PAGE_SIZE], emat_ref[...])
                              for r in range(pg)], axis=1)
    update(_dot_nt(qs, kall), hbias_ref[...] + (spread - 1.0) * (-NEG_BIG), lambda pb: _dot(pb, vall))

    @pl.when(j == pl.num_programs(1) - 1)
    def _():
        bias1 = jnp.where(skn_ref[0] >= thr, 0.0, NEG_BIG)
        hs = lambda ref, h: ref[0, :, h * LANES:(h + 1) * LANES]
        lg = jnp.concatenate([rows(_dot_nt(qs, hs(kn_ref, h)), h) for h in range(H_A)], axis=0)
        update(lg, jnp.concatenate([bias1] * H_A, axis=0),
               lambda pb: jnp.concatenate([rows(_dot(pb, hs(vn_ref, h)), h) for h in range(H_A)], axis=0))
        o = acc_sc[...] / l_sc[:, :1]
        for h in range(H_A):
            out_ref[0, :, h * LANES:(h + 1) * LANES] = rows(o, h).astype(BF16)


def _dsa_sample(aq, iq, sm, ak, av, ik, cache_k, cache_v, cache_kidx, page_table, nb, t):
    n_pages = page_table.shape[1]
    npast = n_pages * PAGE_SIZE
    topk = min(TOPK_MAX, (npast + t) // 4)
    pg = math.gcd(8, n_pages)
    pgs = math.gcd(32, n_pages)
    assert t & (t - 1) == 0 and H_A & (H_A - 1) == 0
    n_pool = cache_k.shape[0]
    ck = cache_k.reshape(n_pool, PAGE_SIZE * H_A, DH_A)
    cv = cache_v.reshape(n_pool, PAGE_SIZE * H_A, DH_A)
    emat = (jnp.arange(PAGE_SIZE * H_A)[None, :] // H_A == jnp.arange(PAGE_SIZE)[:, None]).astype(BF16)
    same_head = (jnp.arange(pg * PAGE_SIZE * H_A)[None, :] % H_A) == (jnp.arange(H_A * t)[:, None] // t)
    hbias = jnp.where(same_head, 0.0, NEG_BIG).astype(F32)
    r3 = lambda a: a.reshape(nb, t, a.shape[-1])
    padrows = lambda a: jnp.pad(r3(a), ((0, 0), (0, LANES - t), (0, 0))).astype(BF16)
    tokspec = lambda w: pl.BlockSpec((1, t, w), lambda b, j, pt: (b, 0, 0))
    newspec = lambda w: pl.BlockSpec((1, LANES, w), lambda b, j, pt: (b, 0, 0))

    def pagespec(rows_, w, r, per_step):
        return pl.BlockSpec((1, rows_, w), lambda b, j, pt, r=r: (pt[b, j * per_step + r], 0, 0))

    skey, skn, thr = pl.pallas_call(
        functools.partial(_dsa_s_score_kernel, pgs=pgs, t=t, topk=topk, npast=npast),
        grid_spec=pltpu.PrefetchScalarGridSpec(
            num_scalar_prefetch=1,
            grid=(nb, n_pages // pgs),
            in_specs=[tokspec(512), tokspec(LANES), newspec(D_IDX)]
                     + [pagespec(PAGE_SIZE, D_IDX, r, pgs) for r in range(pgs)],
            out_specs=[tokspec(npast), tokspec(LANES), tokspec(LANES)],
        ),
        out_shape=[jax.ShapeDtypeStruct((nb, t, npast), I32),
                   jax.ShapeDtypeStruct((nb, t, LANES), I32),
                   jax.ShapeDtypeStruct((nb, t, LANES), I32)],
        compiler_params=_cparams(("parallel", "arbitrary")),
        name="dsa_sample_score",
    )(page_table, r3(iq), r3(sm), padrows(ik), *([cache_kidx] * pgs))

    att = pl.pallas_call(
        functools.partial(_dsa_s_att_kernel, pg=pg, t=t),
        grid_spec=pltpu.PrefetchScalarGridSpec(
            num_scalar_prefetch=1,
            grid=(nb, n_pages // pg),
            in_specs=[tokspec(512),
                      pl.BlockSpec((1, t, pg * PAGE_SIZE), lambda b, j, pt: (b, 0, j)),
                      tokspec(LANES), tokspec(LANES), newspec(512), newspec(512),
                      pl.BlockSpec(emat.shape, lambda b, j, pt: (0, 0)),
                      pl.BlockSpec(hbias.shape, lambda b, j, pt: (0, 0))]
                     + [pagespec(PAGE_SIZE * H_A, DH_A, r, pg) for r in range(pg)] * 2,
            out_specs=tokspec(512),
            scratch_shapes=[pltpu.VMEM((H_A * t, LANES), F32)] * 3,
        ),
        out_shape=jax.ShapeDtypeStruct((nb, t, H_A * DH_A), BF16),
        compiler_params=_cparams(("parallel", "arbitrary")),
        name="dsa_sample_att",
    )(page_table, r3(aq), skey, skn, thr, padrows(ak), padrows(av), emat, hbias, *([ck] * pg), *([cv] * pg))
    return att.reshape(nb * t, H_A * DH_A)


def _merge_kernel(x_ref, hn_ref, att_ref, sg_ref, wa_ref, wb_ref, wo_ref, gf_ref, wpqt_ref, bd_ref,
                  h1_ref, xn2t_ref, st_ref):
    sg = sg_ref[...]
    merged = sg[:, :D_MODEL] * _dot(hn_ref[...], wa_ref[...]) + sg[:, D_MODEL:] * _dot(att_ref[...], wb_ref[...])
    h1 = x_ref[...] + _dot(merged.astype(BF16), wo_ref[...])
    h1_ref[...] = h1
    xn2 = (h1 * lax.rsqrt(jnp.mean(h1 * h1, axis=-1, keepdims=True) + EPS)) * gf_ref[...]
    xn2t = xn2.T.astype(BF16)
    xn2t_ref[...] = xn2t
    qt = _dot(wpqt_ref[...], xn2t).astype(BF16)
    for h in range(H_P):
        st_ref[h * 2 * N_KEYS:(h + 1) * 2 * N_KEYS, :] = _dot(bd_ref[h], qt[h * D_PKEY:(h + 1) * D_PKEY, :])


def _prep_subkeys(sub_keys):
    z = jnp.zeros_like(sub_keys[:, 0])
    top = jnp.concatenate([sub_keys[:, 0], z], axis=-1)
    bot = jnp.concatenate([z, sub_keys[:, 1]], axis=-1)
    return jnp.concatenate([top, bot], axis=1).astype(BF16)


def _merge(x2d, hn, att, sg, wa, wb, wo, g_ffn, wpqt, bd, tm):
    n = x2d.shape[0]
    assert n % tm == 0
    row = lambda w: pl.BlockSpec((tm, w), lambda i: (i, 0))
    full2 = lambda a: pl.BlockSpec(a.shape, lambda i: (0, 0))
    return pl.pallas_call(
        _merge_kernel,
        grid=(n // tm,),
        in_specs=[row(D_MODEL), row(512), row(512), row(2 * D_MODEL), full2(wa), full2(wb), full2(wo),
                  pl.BlockSpec((1, D_MODEL), lambda i: (0, 0)), full2(wpqt),
                  pl.BlockSpec(bd.shape, lambda i: (0, 0, 0))],
        out_specs=[row(D_MODEL), pl.BlockSpec((D_MODEL, tm), lambda i: (0, i)),
                   pl.BlockSpec((H_P * 2 * N_KEYS, tm), lambda i: (0, i))],
        out_shape=[jax.ShapeDtypeStruct((n, D_MODEL), F32), jax.ShapeDtypeStruct((D_MODEL, n), BF16),
                   jax.ShapeDtypeStruct((H_P * 2 * N_KEYS, n), F32)],
        compiler_params=_cparams(("parallel",)),
        name="merge",
    )(x2d, hn, att, sg, wa, wb, wo, g_ffn.reshape(1, D_MODEL), wpqt, bd)


def _extract_top(s, k, first_only=True):
    rows = lax.broadcasted_iota(I32, s.shape, 0)
    cur, vals = s, []
    for _ in range(k):
        m = jnp.max(cur, axis=0, keepdims=True)
        if first_only:
            first = jnp.min(jnp.where(cur == m, rows, s.shape[0]), axis=0, keepdims=True)
            cur = jnp.where(rows == first, -jnp.inf, cur)
        else:
            cur = jnp.where(cur == m, -jnp.inf, cur)
        vals.append(m)
    return vals, cur


_STAIR = [(a, b) for a in range(P_TOPK) for b in range(P_TOPK) if (a + 1) * (b + 1) <= P_TOPK]


def _peer_prep_kernel(st_ref, s1m_ref, s2m_ref, e1_ref, e2_ref, tau_ref, top_ref, sel_ref):
    tn = st_ref.shape[1]
    taus = []

    def top_keys(rows, slot):
        def run(first_only):
            s = st_ref[rows, :]
            vals, res = _extract_top(s, P_TOPK, first_only)
            picked = jnp.where(res != s, 1.0, 0.0)
            top_ref[slot] = jnp.concatenate(vals, axis=0)
            sel_ref[slot] = picked
            return picked
        picked = run(False)
        npick = jnp.sum(picked, axis=0, keepdims=True)
        bad = jnp.sum(jnp.where(npick == float(P_TOPK), 0.0, 1.0))

        @pl.when(bad > 0.0)
        def _():
            run(True)

    for h in range(H_P):
        r1s = slice((2 * h) * N_KEYS, (2 * h + 1) * N_KEYS)
        r2s = slice((2 * h + 1) * N_KEYS, (2 * h + 2) * N_KEYS)
        top_keys(r1s, 0)
        top_keys(r2s, 1)
        s1, s2 = st_ref[r1s, :], st_ref[r2s, :]
        sel1, sel2 = sel_ref[0] > 0.0, sel_ref[1] > 0.0
        v1 = [top_ref[0, a:a + 1, :] for a in range(P_TOPK)]
        v2 = [top_ref[1, a:a + 1, :] for a in range(P_TOPK)]
        cand = [v1[a] + v2[b] for a, b in _STAIR]
        npad = (-len(cand)) % 8
        cand = jnp.concatenate(cand + [jnp.full((npad, tn), -jnp.inf, F32)], axis=0)
        top, _ = _extract_top(cand, P_TOPK)
        z = jnp.zeros_like(top[0])
        for c in top:
            z = z + jnp.exp(c - top[0])
        taus.append(top[P_TOPK - 1])
        sl = slice(h * N_KEYS, (h + 1) * N_KEYS)
        outs = ((s1m_ref, jnp.where(sel1, s1, -jnp.inf)), (s2m_ref, jnp.where(sel2, s2, -jnp.inf)),
                (e1_ref, jnp.where(sel1, jnp.exp(s1 - v1[0]), 0.0) / z),
                (e2_ref, jnp.where(sel2, jnp.exp(s2 - v2[0]), 0.0)))
        for ref, val in outs:
            for lt in range(tn // LANES):
                ref[lt, sl, :] = val[:, lt * LANES:(lt + 1) * LANES]
    tau = jnp.concatenate(taus, axis=0)
    for lt in range(tn // LANES):
        tau_ref[lt] = tau[:, lt * LANES:(lt + 1) * LANES]


def _peer_prep(st, tn):
    n = st.shape[1]
    assert n % tn == 0 and tn % LANES == 0
    hk = H_P * N_KEYS
    col = lambda r: pl.BlockSpec((tn // LANES, r, LANES), lambda i: (i, 0, 0))
    return pl.pallas_call(
        _peer_prep_kernel,
        grid=(n // tn,),
        in_specs=[pl.BlockSpec((2 * hk, tn), lambda i: (0, i))],
        out_specs=[col(hk), col(hk), col(hk), col(hk), col(H_P)],
        out_shape=[jax.ShapeDtypeStruct((n // LANES, hk, LANES), F32)] * 4
                  + [jax.ShapeDtypeStruct((n // LANES, H_P, LANES), F32)],
        scratch_shapes=[pltpu.VMEM((2, P_TOPK, tn), F32), pltpu.VMEM((2, N_KEYS, tn), F32)],
        compiler_params=_cparams(("parallel",)),
        name="peer_prep",
    )(st)


def _peer_dense_kernel(u_ref, vt_ref, xn2t_ref, s1m_ref, s2m_ref, e1_ref, e2_ref, tau_ref, h1_ref, gf_ref,
                       y_ref, acc_ref, a_ref, yb_ref):
    e = pl.program_id(1)
    te, tn = u_ref.shape[0], xn2t_ref.shape[1]
    nblk, nlt = te // N_KEYS, tn // LANES

    @pl.when(e == 0)
    def _():
        acc_ref[...] = jnp.zeros(acc_ref.shape, F32)

    a = _dot(u_ref[...], xn2t_ref[...])
    for lt in range(nlt):
        a_ref[lt] = a[:, lt * LANES:(lt + 1) * LANES]

    def token_tile(lt, carry):
        def expert_tile(blk, carry2):
            b0 = pl.multiple_of(blk * N_KEYS, N_KEYS)
            a_idx = e * nblk + blk
            w = jnp.zeros((N_KEYS, LANES), F32)
            for h in range(H_P):
                sl = slice(h * N_KEYS, (h + 1) * N_KEYS)
                s1row = s1m_ref[lt, pl.ds(h * N_KEYS + a_idx, 1), :]
                e1row = e1_ref[lt, pl.ds(h * N_KEYS + a_idx, 1), :]
                t = s2m_ref[lt, sl, :] + s1row
                w = w + jnp.where(t >= tau_ref[lt, h:h + 1, :], e2_ref[lt, sl, :], 0.0) * e1row
            a_t = a_ref[lt, pl.ds(b0, N_KEYS), :]
            gl = 0.5 * a_t * (1.0 + lax.erf(a_t * (2.0 ** -0.5)))
            yb_ref[lt, pl.ds(b0, N_KEYS), :] = (w * gl).astype(BF16)
            return carry2

        lax.fori_loop(0, nblk, expert_tile, 0)
        return carry

    lax.fori_loop(0, nlt, token_tile, 0)
    yb = jnp.concatenate([yb_ref[lt] for lt in range(nlt)], axis=1)
    acc_ref[...] += _dot(vt_ref[...], yb)

    @pl.when(e == pl.num_programs(1) - 1)
    def _():
        h2 = h1_ref[...] + acc_ref[...].T
        y_ref[...] = (h2 * lax.rsqrt(jnp.mean(h2 * h2, axis=-1, keepdims=True) + EPS)) * gf_ref[...]


def _peer_dense(ub, vtb, xn2t, s1m, s2m, e1, e2, tau, h1, g_final, tn, te):
    n = xn2t.shape[1]
    ne = ub.shape[0]
    assert n % tn == 0 and ne % te == 0 and tn % LANES == 0 and te % N_KEYS == 0
    hk = H_P * N_KEYS
    col = lambda r: pl.BlockSpec((tn // LANES, r, LANES), lambda i, e: (i, 0, 0))
    row = lambda w: pl.BlockSpec((tn, w), lambda i, e: (i, 0))
    return pl.pallas_call(
        _peer_dense_kernel,
        grid=(n // tn, ne // te),
        in_specs=[pl.BlockSpec((te, D_MODEL), lambda i, e: (e, 0)),
                  pl.BlockSpec((D_MODEL, te), lambda i, e: (0, e)),
                  pl.BlockSpec((D_MODEL, tn), lambda i, e: (0, i)),
                  col(hk), col(hk), col(hk), col(hk), col(H_P), row(D_MODEL),
                  pl.BlockSpec((1, D_MODEL), lambda i, e: (0, 0))],
        out_specs=row(D_MODEL),
        out_shape=jax.ShapeDtypeStruct((n, D_MODEL), F32),
        scratch_shapes=[pltpu.VMEM((D_MODEL, tn), F32), pltpu.VMEM((tn // LANES, te, LANES), F32),
                        pltpu.VMEM((tn // LANES, te, LANES), BF16)],
        compiler_params=_cparams(("parallel", "arbitrary")),
        name="peer_dense",
    )(ub, vtb, xn2t, s1m, s2m, e1, e2, tau, h1, g_final.reshape(1, D_MODEL))


def _row_tile(n, pref):
    t = pref
    while n % t:
        t //= 2
    return t


def _group(x, pos, tables_rows, attend, c0, n0, m0, lw, fw, g_final):
    nb, t, _ = x.shape
    n = nb * t
    g_mix, wp, bias_row, g_mnorm, wa, wb, wo = lw
    g_ffn, wpqt, bd, ub, vtb = fw
    x2d = x.reshape(n, D_MODEL)
    tm = _row_tile(n, 256)
    tables = _rope_tables(pos)
    if tables_rows != t:
        tables = tuple(jnp.tile(a, (tables_rows // t, 1)) for a in tables)
    (mq, mk, mv, mo, aq, ak, av, akb, avb, iq, sg, sm, ik, ik2,
     aqt, avt, iqt, smt) = _inproj(x2d, g_mix, wp, bias_row, tables, tm)
    hn, cc, nn, mm = _mlstm(mq, mk, mv, sm, mo, g_mnorm, c0, n0, m0, nb, t)
    att = attend(dict(aq=aq, iq=iq, sm=sm, ak=ak, av=av, akb=akb, ik=ik, ik2=ik2,
                      aqt=aqt, avt=avt, iqt=iqt, smt=smt))
    h1, xn2t, st = _merge(x2d, hn, att, sg, wa, wb, wo, g_ffn, wpqt, bd, tm)
    tn = _row_tile(n, 512)
    s1m, s2m, e1, e2, tau = _peer_prep(st, _row_tile(n, 256))
    y = _peer_dense(ub, vtb, xn2t, s1m, s2m, e1, e2, tau, h1, g_final, tn, 512)
    return (y.reshape(nb, t, D_MODEL), ak.reshape(nb, t, H_A, DH_A), av.reshape(nb, t, H_A, DH_A),
            ik.reshape(nb, t, D_IDX), cc, nn, mm)


def kernel(x_prompt, x_sample, cache_k, cache_v, cache_kidx, page_table, state_C, state_n, state_m,
           g_mix, w_in, b_mgate, g_mnorm, w_a, w_b, w_o, g_ffn, w_pq, sub_keys, peer_u, peer_v, g_final):
    depth = w_in.shape[0]
    assert depth == 1, "single-layer step"
    l = 0
    bp, sp, _ = x_prompt.shape
    bd_, td, _ = x_sample.shape
    past = page_table.shape[1] * PAGE_SIZE

    one = lambda a: a.reshape(a.shape[1:])
    bias_row = jnp.zeros((1, LANES), F32).at[0, SM_IP:SM_IW].set(b_mgate[l].astype(F32))
    lw = (g_mix[l], _prep_w_in(one(w_in)), bias_row, g_mnorm[l],
          one(w_a).astype(BF16), one(w_b).astype(BF16), one(w_o).astype(BF16))
    fw = (g_ffn[l], one(w_pq).T.astype(BF16), _prep_subkeys(one(sub_keys)),
          one(peer_u).astype(BF16), _transpose_cast(one(peer_v), 512))

    def attend_p(p):
        return _dsa_prompt_t(p["aqt"], p["iqt"], p["smt"], p["akb"], p["avt"], p["ik2"], bp, sp)

    def attend_s(p):
        return _dsa_sample(p["aq"], p["iq"], p["sm"], p["ak"], p["av"], p["ik"],
                           one(cache_k), one(cache_v), one(cache_kidx), page_table, bd_, td)

    zc = jnp.zeros((bp, H_M, DV_M, DK_M), F32)
    zn = jnp.zeros((bp, H_M, DK_M), F32)
    zm = jnp.zeros((bp, H_M), F32)
    yp, kp, vp, kip, cp, np_, mp = _group(x_prompt, jnp.arange(sp), sp, attend_p, zc, zn, zm, lw, fw, g_final)
    ys, ks, vs, kis, cs, ns, ms = _group(x_sample, past + jnp.arange(td), bd_ * td, attend_s,
                                         one(state_C), one(state_n), one(state_m), lw, fw, g_final)
    st = lambda a: a[None]
    return (yp, ys, st(kp), st(vp), st(kip), st(cp), st(np_), st(mp),
            st(ks), st(vs), st(kis), st(cs), st(ns), st(ms))
```
